```python
import math, functools
import jax, jax.numpy as jnp
from jax import lax
import numpy as np

D_MODEL = 2048
BATCH = 4
SEQ = 2048
DEPTH = 2
DEC_BATCH = 128
DEC_SEQ = 8
PAST_LEN = 2048
PAGE_SIZE = 128

HEAD_DIM = 128
NSA_HEADS = 8
NSA_KV_HEADS = 2
NSA_GROUP = NSA_HEADS // NSA_KV_HEADS
CMP_BLOCK = 32
SEL_BLOCK = 64
SEL_TOPK = 16
WINDOW = 512
DIFF_HEADS = 4
DIFF_QK = 128
DIFF_V = 2 * DIFF_QK
NSA_WIDTH = NSA_HEADS * HEAD_DIM
DIFF_WIDTH = DIFF_HEADS * DIFF_V
D_FF = 5504
CONV_W = 3
ROPE_THETA = 10000.0
EPS = 1e-6
Q_BLOCK = 128
NEG = -1e30
FORCE = 100.0

N_Q_NSA = NSA_HEADS * HEAD_DIM
N_KV_NSA = 6 * NSA_KV_HEADS * HEAD_DIM
N_G_NSA = 3 * NSA_HEADS
N_Q_DIFF = DIFF_HEADS * 2 * DIFF_QK
N_K_DIFF = DIFF_HEADS * 2 * DIFF_QK
N_V_DIFF = DIFF_HEADS * DIFF_V
N_MERGE = 2 * D_MODEL
D_IN = N_Q_NSA + N_KV_NSA + N_G_NSA + N_Q_DIFF + N_K_DIFF + N_V_DIFF + N_MERGE

kernel_name = 'nsa_diffattn_convffn_hybrid_step'


def rms_norm(x, g):
    xf = x.astype(jnp.float32)
    y = xf * lax.rsqrt(jnp.mean(xf * xf, axis=-1, keepdims=True) + EPS)
    return (y * g.astype(jnp.float32)).astype(x.dtype)


def rope(x, pos):
    half = x.shape[-1] // 2
    freqs = ROPE_THETA ** (-jnp.arange(half, dtype=jnp.float32) / half)
    ang = pos.astype(jnp.float32)[:, None] * freqs
    ang = ang.reshape((1, ang.shape[0]) + (1,) * (x.ndim - 3) + (half,))
    cos, sin = jnp.cos(ang), jnp.sin(ang)
    xf = x.astype(jnp.float32)
    x1, x2 = xf[..., :half], xf[..., half:]
    return jnp.concatenate([x1 * cos - x2 * sin, x2 * cos + x1 * sin], axis=-1).astype(x.dtype)


def masked_softmax(s, mask):
    s = jnp.where(mask, s.astype(jnp.float32), NEG)
    return jax.nn.softmax(s, axis=-1) * mask


def ada_modulation(c, w_ada, b_ada):
    mod = jnp.einsum('bd,de->be', jax.nn.silu(c), w_ada) + b_ada
    return [m[:, None, :] for m in jnp.split(mod, 6, axis=-1)]


def mixer_inputs(h, w_in, pos, g_nsa_q, g_nsa_k, g_diff_q, g_diff_k):
    B, T, _ = h.shape
    z = jnp.einsum('btd,de->bte', h, w_in)
    cuts = []
    acc = 0
    for n in (N_Q_NSA, N_KV_NSA, N_G_NSA, N_Q_DIFF, N_K_DIFF, N_V_DIFF):
        acc += n
        cuts.append(acc)
    q_nsa, kv_nsa, g_nsa, q_diff, k_diff, v_diff, g_mrg = jnp.split(z, cuts, axis=-1)
    q_nsa = rope(rms_norm(q_nsa.reshape(B, T, NSA_HEADS, HEAD_DIM), g_nsa_q), pos)
    kv = kv_nsa.reshape(B, T, 6, NSA_KV_HEADS, HEAD_DIM)
    slc_k = rope(rms_norm(kv[:, :, 2], g_nsa_k[1]), pos)
    win_k = rope(rms_norm(kv[:, :, 4], g_nsa_k[2]), pos)
    nsa_rows = jnp.stack([kv[:, :, 0], kv[:, :, 1], slc_k, kv[:, :, 3]], axis=2)
    win_rows = jnp.stack([win_k, kv[:, :, 5]], axis=2)
    g_nsa = jax.nn.sigmoid(g_nsa.reshape(B, T, NSA_HEADS, 3))
    q_diff = rope(rms_norm(q_diff.reshape(B, T, DIFF_HEADS, 2, DIFF_QK), g_diff_q), pos)
    k_diff = rope(rms_norm(k_diff.reshape(B, T, DIFF_HEADS, 2, DIFF_QK), g_diff_k), pos)
    diff_rows = jnp.stack([k_diff.reshape(B, T, DIFF_HEADS, 2 * DIFF_QK),
                           v_diff.reshape(B, T, DIFF_HEADS, DIFF_V)], axis=2)
    g_mrg = jax.nn.sigmoid(g_mrg.reshape(B, T, 2, D_MODEL))
    return q_nsa, g_nsa, q_diff, nsa_rows, win_rows, diff_rows, g_mrg


def compress(rows_k, rows_v, pe, w_phi, g_ck):
    nc = rows_k.shape[0] // CMP_BLOCK
    shp = (nc, CMP_BLOCK, NSA_KV_HEADS, HEAD_DIM)
    zk = jnp.einsum('cjgd,jd->cgd', rows_k.reshape(shp), pe[0]) / CMP_BLOCK
    zv = jnp.einsum('cjgd,jd->cgd', rows_v.reshape(shp), pe[1]) / CMP_BLOCK
    ck = jnp.einsum('cgd,de->cge', zk, w_phi[0])
    cv = jnp.einsum('cgd,de->cge', zv, w_phi[1])
    cpos = jnp.arange(nc) * CMP_BLOCK + (CMP_BLOCK - 1)
    ck = rope(rms_norm(ck, g_ck)[None], cpos)[0]
    return ck, cv, cpos


def attend_block(qa, ga, qb, pos_q, ck, cv, cpos, sk, sv, wk, wv, wpos, dk, dv, lam, lam_init, g_dout):
    Q = qa.shape[0]
    scale = HEAD_DIM ** -0.5
    qg = qa.reshape(Q, NSA_KV_HEADS, NSA_GROUP, HEAD_DIM)
    t = pos_q[:, None]
    s = jnp.einsum('qghd,cgd->qghc', qg, ck) * scale
    p_c = masked_softmax(s, (cpos[None, :] <= t)[:, None, None, :])
    o_c = jnp.einsum('qghc,cgd->qghd', p_c.astype(cv.dtype), cv)
    ns = sk.shape[1]
    imp = p_c.sum(axis=2).reshape(Q, NSA_KV_HEADS, ns, SEL_BLOCK // CMP_BLOCK).sum(-1)
    blk = jnp.arange(ns)[None, :]
    cur = t // SEL_BLOCK
    forced = (blk == 0) | (blk == cur) | (blk == cur - 1)
    valid = blk * SEL_BLOCK <= t
    score = jnp.where(valid[:, None, :], imp + FORCE * forced[:, None, :], -1.0)
    top_s, idx = lax.top_k(score, min(SEL_TOPK, ns))
    gather = jax.vmap(lambda blocks, ix: blocks[ix], in_axes=(0, 1), out_axes=1)
    ks = gather(sk, idx)
    vs = gather(sv, idx)
    n_sel = idx.shape[-1] * SEL_BLOCK
    kpos = idx[..., None] * SEL_BLOCK + jnp.arange(SEL_BLOCK)
    m_s = (kpos <= pos_q[:, None, None, None]) & (top_s >= 0)[..., None]
    s = jnp.einsum('qghd,qgksd->qghks', qg, ks).reshape(Q, NSA_KV_HEADS, NSA_GROUP, n_sel) * scale
    p_s = masked_softmax(s, m_s.reshape(Q, NSA_KV_HEADS, 1, n_sel))
    o_s = jnp.einsum('qghn,qgnd->qghd', p_s.astype(vs.dtype), vs.reshape(Q, NSA_KV_HEADS, n_sel, HEAD_DIM))
    dist = t - wpos[None, :]
    m_w = (dist >= 0) & (dist < WINDOW) & (wpos[None, :] >= 0)
    s = jnp.einsum('qghd,wgd->qghw', qg, wk) * scale
    p_w = masked_softmax(s, m_w[:, None, None, :])
    o_w = jnp.einsum('qghw,wgd->qghd', p_w.astype(wv.dtype), wv)
    gg = ga.reshape(Q, NSA_KV_HEADS, NSA_GROUP, 3)
    o_nsa = (gg[..., 0:1] * o_c + gg[..., 1:2] * o_s + gg[..., 2:3] * o_w).reshape(Q, NSA_WIDTH)
    kd = dk.reshape(dk.shape[0], DIFF_HEADS, 2, DIFF_QK)
    s = jnp.einsum('qhid,khid->hiqk', qb, kd) * (DIFF_QK ** -0.5)
    m_d = jnp.arange(dk.shape[0])[None, :] <= t
    p_d = masked_softmax(s, m_d[None, None])
    a = p_d[:, 0] - lam * p_d[:, 1]
    o_d = jnp.einsum('hqk,khe->qhe', a.astype(dv.dtype), dv)
    o_d = (rms_norm(o_d, g_dout) * (1.0 - lam_init)).reshape(Q, DIFF_WIDTH)
    return jnp.concatenate([o_nsa, o_d], axis=-1)


def prompt_attention(q_nsa, g_nsa, q_diff, nsa_rows, win_rows, diff_rows, lp, lam, lam_init):
    def one_seq(args):
        qa, ga, qb, rows, wrows, drows = args
        T = qa.shape[0]
        ck, cv, cpos = compress(rows[:, 0], rows[:, 1], lp['pe_cmp'], lp['w_phi'], lp['g_nsa_k'][0])
        sk = rows[:, 2].reshape(T // SEL_BLOCK, SEL_BLOCK, NSA_KV_HEADS, HEAD_DIM).transpose(2, 0, 1, 3)
        sv = rows[:, 3].reshape(T // SEL_BLOCK, SEL_BLOCK, NSA_KV_HEADS, HEAD_DIM).transpose(2, 0, 1, 3)
        wpad = jnp.pad(wrows, ((WINDOW, 0), (0, 0), (0, 0), (0, 0)))

        def one_block(i):
            s0 = i * Q_BLOCK
            qa_b = lax.dynamic_slice_in_dim(qa, s0, Q_BLOCK, axis=0)
            ga_b = lax.dynamic_slice_in_dim(ga, s0, Q_BLOCK, axis=0)
            qb_b = lax.dynamic_slice_in_dim(qb, s0, Q_BLOCK, axis=0)
            w = lax.dynamic_slice_in_dim(wpad, s0, WINDOW + Q_BLOCK, axis=0)
            wpos = s0 - WINDOW + jnp.arange(WINDOW + Q_BLOCK)
            return attend_block(qa_b, ga_b, qb_b, s0 + jnp.arange(Q_BLOCK), ck, cv, cpos, sk, sv,
                                w[:, 0], w[:, 1], wpos, drows[:, 0], drows[:, 1], lam, lam_init, lp['g_diff_out'])

        out = lax.map(one_block, jnp.arange(T // Q_BLOCK))
        return out.reshape(T, NSA_WIDTH + DIFF_WIDTH)

    return lax.map(one_seq, (q_nsa, g_nsa, q_diff, nsa_rows, win_rows, diff_rows))


def sample_attention(q_nsa, g_nsa, q_diff, nsa_rows, win_rows, diff_rows,
                     cache_nsa, cache_diff, win_buf, page_table, layer, lp, lam, lam_init):
    S = q_nsa.shape[1]
    past_len = page_table.shape[1] * cache_nsa.shape[2]
    w_buf = win_buf.shape[1]
    pad = (-(past_len + S)) % SEL_BLOCK
    l_pad = past_len + S + pad
    pos_q = past_len + jnp.arange(S)
    wpos = past_len - w_buf + jnp.arange(w_buf + S)

    def one_seq(args):
        qa, ga, qb, rows, wrows, drows, pt, wb = args
        past = cache_nsa[layer, pt].reshape((past_len,) + cache_nsa.shape[3:])
        allr = jnp.pad(jnp.concatenate([past, rows], axis=0), ((0, pad), (0, 0), (0, 0), (0, 0)))
        ck, cv, cpos = compress(allr[:, 0], allr[:, 1], lp['pe_cmp'], lp['w_phi'], lp['g_nsa_k'][0])
        sk = allr[:, 2].reshape(l_pad // SEL_BLOCK, SEL_BLOCK, NSA_KV_HEADS, HEAD_DIM).transpose(2, 0, 1, 3)
        sv = allr[:, 3].reshape(l_pad // SEL_BLOCK, SEL_BLOCK, NSA_KV_HEADS, HEAD_DIM).transpose(2, 0, 1, 3)
        w = jnp.concatenate([wb, wrows], axis=0)
        dpast = cache_diff[layer, pt].reshape((past_len,) + cache_diff.shape[3:])
        dall = jnp.concatenate([dpast, drows], axis=0)
        return attend_block(qa, ga, qb, pos_q, ck, cv, cpos, sk, sv, w[:, 0], w[:, 1], wpos,
                            dall[:, 0], dall[:, 1], lam, lam_init, lp['g_diff_out'])

    return lax.map(one_seq, (q_nsa, g_nsa, q_diff, nsa_rows, win_rows, diff_rows, page_table, win_buf))


def merge_branches(attn, g_mrg, w_branch, w_out):
    oa = jnp.einsum('btc,cd->btd', attn[..., :NSA_WIDTH], w_branch[0])
    ob = jnp.einsum('btc,cd->btd', attn[..., NSA_WIDTH:], w_branch[1])
    return jnp.einsum('btd,de->bte', g_mrg[:, :, 0] * oa + g_mrg[:, :, 1] * ob, w_out)


def conv_ffn(h, prev, w_up, conv_w, conv_b, w_down):
    T = h.shape[1]
    ug, uv = jnp.split(jnp.einsum('btd,df->btf', h, w_up), 2, axis=-1)
    full = jnp.concatenate([prev.astype(ug.dtype), ug], axis=1)
    conv = conv_b + full[:, 0:T] * conv_w[0]
    for j in range(1, CONV_W):
        conv = conv + full[:, j:j + T] * conv_w[j]
    y = jnp.einsum('btf,fd->btd', jax.nn.gelu(conv) * uv, w_down)
    return y, full[:, full.shape[1] - (CONV_W - 1):]


def run_layer(x, c, pos, attend, conv_prev, lp):
    sh1, sc1, gt1, sh2, sc2, gt2 = ada_modulation(c, lp['w_ada'], lp['b_ada'])
    h = rms_norm(x, lp['norm1_g']) * (1.0 + sc1) + sh1
    q_nsa, g_nsa, q_diff, nsa_rows, win_rows, diff_rows, g_mrg = mixer_inputs(
        h, lp['w_in'], pos, lp['g_nsa_q'], lp['g_nsa_k'], lp['g_diff_q'], lp['g_diff_k'])
    attn = attend(q_nsa, g_nsa, q_diff, nsa_rows, win_rows, diff_rows)
    x = x + gt1 * merge_branches(attn, g_mrg, lp['w_branch'], lp['w_out'])
    h2 = rms_norm(x, lp['norm2_g']) * (1.0 + sc2) + sh2
    f, conv_state = conv_ffn(h2, conv_prev, lp['w_up'], lp['conv_w'], lp['conv_b'], lp['w_down'])
    return x + gt2 * f, nsa_rows, diff_rows, win_rows, conv_state


def setup_inputs(seed: int = 0) -> dict:
    key = jax.random.key(seed)
    ks = jax.random.split(key, 32)
    n_pages = PAST_LEN // PAGE_SIZE
    n_used = DEC_BATCH * n_pages
    n_pool = n_used + max(1, n_used // 4)
    w_buf = min(WINDOW, PAST_LEN)

    def nrm(k, shape, s=1.0):
        return jax.random.normal(k, shape, jnp.float32) * s

    def gain(k, shape, s=0.02):
        return 1.0 + nrm(k, shape, s)

    page_table = jax.random.permutation(ks[0], n_pool)[:n_used].reshape(DEC_BATCH, n_pages).astype(jnp.int32)
    return {
        'x_prompt': nrm(ks[1], (BATCH, SEQ, D_MODEL)),
        'x_sample': nrm(ks[2], (DEC_BATCH, DEC_SEQ, D_MODEL)),
        'cache_nsa_kv': nrm(ks[3], (DEPTH, n_pool, PAGE_SIZE, 4, NSA_KV_HEADS, HEAD_DIM)),
        'cache_diff_kv': nrm(ks[4], (DEPTH, n_pool, PAGE_SIZE, 2, DIFF_HEADS, 2 * DIFF_QK)),
        'state_win_kv': nrm(ks[5], (DEPTH, DEC_BATCH, w_buf, 2, NSA_KV_HEADS, HEAD_DIM)),
        'state_conv': nrm(ks[6], (DEPTH, DEC_BATCH, CONV_W - 1, D_FF)),
        'page_table': page_table,
        'c_prompt': nrm(ks[7], (BATCH, D_MODEL)),
        'c_sample': nrm(ks[8], (DEC_BATCH, D_MODEL)),
        'w_ada': nrm(ks[9], (DEPTH, D_MODEL, 6 * D_MODEL), 0.5 * D_MODEL ** -0.5),
        'b_ada': nrm(ks[10], (DEPTH, 6 * D_MODEL), 0.01),
        'norm1_g': gain(ks[11], (DEPTH, D_MODEL)),
        'norm2_g': gain(ks[12], (DEPTH, D_MODEL)),
        'w_in': nrm(ks[13], (DEPTH, D_MODEL, D_IN), D_MODEL ** -0.5),
        'g_nsa_q': gain(ks[14], (DEPTH, HEAD_DIM)),
        'g_nsa_k': gain(ks[15], (DEPTH, 3, HEAD_DIM)),
        'pe_cmp': gain(ks[16], (DEPTH, 2, CMP_BLOCK, HEAD_DIM), 0.1),
        'w_phi': nrm(ks[17], (DEPTH, 2, HEAD_DIM, HEAD_DIM), HEAD_DIM ** -0.5),
        'g_diff_q': gain(ks[18], (DEPTH, DIFF_QK)),
        'g_diff_k': gain(ks[19], (DEPTH, DIFF_QK)),
        'lam_diff': nrm(ks[20], (DEPTH, 4, DIFF_QK), 0.1),
        'g_diff_out': gain(ks[21], (DEPTH, DIFF_V)),
        'w_branch': nrm(ks[22], (DEPTH, 2, NSA_WIDTH, D_MODEL), NSA_WIDTH ** -0.5),
        'w_out': nrm(ks[23], (DEPTH, D_MODEL, D_MODEL), D_MODEL ** -0.5),
        'w_up': nrm(ks[24], (DEPTH, D_MODEL, 2 * D_FF), D_MODEL ** -0.5),
        'conv_w': nrm(ks[25], (DEPTH, CONV_W, D_FF), CONV_W ** -0.5),
        'conv_b': nrm(ks[26], (DEPTH, D_FF), 0.01),
        'w_down': nrm(ks[27], (DEPTH, D_FF, D_MODEL), D_FF ** -0.5),
    }


def reference(x_prompt, x_sample, cache_nsa_kv, cache_diff_kv, state_win_kv, state_conv, page_table,
              c_prompt, c_sample, w_ada, b_ada, norm1_g, norm2_g, w_in, g_nsa_q, g_nsa_k, pe_cmp, w_phi,
              g_diff_q, g_diff_k, lam_diff, g_diff_out, w_branch, w_out, w_up, conv_w, conv_b, w_down):
    B, T, _ = x_prompt.shape
    S = x_sample.shape[1]
    past_len = page_table.shape[1] * cache_nsa_kv.shape[2]
    pos_p = jnp.arange(T)
    pos_s = past_len + jnp.arange(S)
    w_keep_p = min(WINDOW, T)
    xp, xs = x_prompt, x_sample
    nsa_p, diff_p, win_p, conv_p = [], [], [], []
    nsa_s, diff_s, win_s, conv_s = [], [], [], []
    for l in range(DEPTH):
        lp = dict(w_ada=w_ada[l], b_ada=b_ada[l], norm1_g=norm1_g[l], norm2_g=norm2_g[l], w_in=w_in[l],
                  g_nsa_q=g_nsa_q[l], g_nsa_k=g_nsa_k[l], pe_cmp=pe_cmp[l], w_phi=w_phi[l],
                  g_diff_q=g_diff_q[l], g_diff_k=g_diff_k[l], g_diff_out=g_diff_out[l],
                  w_branch=w_branch[l], w_out=w_out[l], w_up=w_up[l], conv_w=conv_w[l],
                  conv_b=conv_b[l], w_down=w_down[l])
        lam_init = 0.8 - 0.6 * math.exp(-0.3 * l)
        lv = lam_diff[l].astype(jnp.float32)
        lam = jnp.exp(jnp.sum(lv[0] * lv[1])) - jnp.exp(jnp.sum(lv[2] * lv[3])) + lam_init
        attend_p = functools.partial(prompt_attention, lp=lp, lam=lam, lam_init=lam_init)
        xp, r_nsa, r_diff, r_win, r_conv = run_layer(
            xp, c_prompt, pos_p, attend_p, jnp.zeros((B, CONV_W - 1, D_FF), x_prompt.dtype), lp)
        nsa_p.append(r_nsa)
        diff_p.append(r_diff)
        win_p.append(r_win[:, T - w_keep_p:])
        conv_p.append(r_conv)
        attend_s = functools.partial(sample_attention, cache_nsa=cache_nsa_kv, cache_diff=cache_diff_kv,
                                     win_buf=state_win_kv[l], page_table=page_table, layer=l, lp=lp,
                                     lam=lam, lam_init=lam_init)
        xs, r_nsa, r_diff, r_win, r_conv = run_layer(xs, c_sample, pos_s, attend_s, state_conv[l], lp)
        nsa_s.append(r_nsa)
        diff_s.append(r_diff)
        win_s.append(jnp.concatenate([state_win_kv[l], r_win], axis=1)[:, S:])
        conv_s.append(r_conv)
    return (xp, xs,
            jnp.stack(nsa_p), jnp.stack(diff_p), jnp.stack(win_p), jnp.stack(conv_p),
            jnp.stack(nsa_s), jnp.stack(diff_s), jnp.stack(win_s), jnp.stack(conv_s))
```

```python
import collections
import functools
import math

import jax
import jax.numpy as jnp
from jax import lax
from jax.experimental import pallas as pl
from jax.experimental.pallas import tpu as pltpu

F32 = jnp.float32
BF16 = jnp.bfloat16

HEAD_DIM = 128
NSA_HEADS = 8
NSA_KV_HEADS = 2
NSA_GROUP = NSA_HEADS // NSA_KV_HEADS
CMP_BLOCK = 32
SEL_BLOCK = 64
SEL_TOPK = 16
WINDOW = 512
DIFF_HEADS = 4
DIFF_QK = 128
DIFF_V = 2 * DIFF_QK
CONV_W = 3
ROPE_THETA = 10000.0
EPS = 1e-6
NEG = -1e30
FORCE = 100.0

LANE = 128
VMEM_LIMIT = 56 * 1024 * 1024
TM = 512
TN_FF = 512
NCMP_PAD = 128

Rows = collections.namedtuple("Rows", "m tm bb tt mod_div")


def _params(sem):
    return pltpu.CompilerParams(dimension_semantics=sem, vmem_limit_bytes=VMEM_LIMIT)


def _rows_bcast(v, rows, width):
    if rows.bb == 1:
        return v.reshape(1, width)
    return jnp.broadcast_to(v, (rows.bb, rows.tt, width)).reshape(rows.tm, width)


def _rms(y, gain):
    return y * lax.rsqrt(jnp.mean(y * y, axis=-1, keepdims=True) + EPS) * gain


def _rope(y, cos, sin_signed):
    return y * cos + pltpu.roll(y, HEAD_DIM // 2, 1) * sin_signed


def _dot(a, b):
    return jnp.dot(a, b, preferred_element_type=F32)


def _dot_t(a, b):
    return lax.dot_general(a, b, (((1,), (1,)), ((), ())), preferred_element_type=F32)


def _ada_kernel(c_ref, w_ref, b_ref, o_ref):
    c = c_ref[...]
    a = (c * jax.nn.sigmoid(c)).astype(BF16)
    o_ref[...] = _dot(a, w_ref[...].astype(BF16)) + b_ref[...]


def _ada(c_all, w_ada, b_ada3, layer):
    mb, d = c_all.shape
    n = w_ada.shape[2]
    tn = 1024
    return pl.pallas_call(
        _ada_kernel,
        grid=(n // tn,),
        in_specs=[pl.BlockSpec((mb, d), lambda j: (0, 0)),
                  pl.BlockSpec((None, d, tn), lambda j: (layer, 0, j)),
                  pl.BlockSpec((None, 1, tn), lambda j: (layer, 0, j))],
        out_specs=pl.BlockSpec((mb, tn), lambda j: (0, j)),
        out_shape=jax.ShapeDtypeStruct((mb, n), F32),
        compiler_params=_params(("arbitrary",)),
        name="ada_modulation",
    )(c_all, w_ada, b_ada3)


def _norm_mod_kernel(rows, x_ref, g_ref, sh_ref, sc_ref, o_ref):
    d = x_ref.shape[1]
    y = _rms(x_ref[...], g_ref[...])
    sc = _rows_bcast(sc_ref[...], rows, d)
    sh = _rows_bcast(sh_ref[...], rows, d)
    o_ref[...] = (y * (1.0 + sc) + sh).astype(o_ref.dtype)


def _norm_mod(x2, gain3, mod3, layer, comp, rows):
    m, d = x2.shape
    return pl.pallas_call(
        functools.partial(_norm_mod_kernel, rows),
        grid=(m // rows.tm,),
        in_specs=[pl.BlockSpec((rows.tm, d), lambda i: (i, 0)),
                  pl.BlockSpec((None, 1, d), lambda i: (layer, 0, 0)),
                  pl.BlockSpec((rows.bb, 1, d), lambda i: (i // rows.mod_div, 0, comp)),
                  pl.BlockSpec((rows.bb, 1, d), lambda i: (i // rows.mod_div, 0, comp + 1))],
        out_specs=pl.BlockSpec((rows.tm, d), lambda i: (i, 0)),
        out_shape=jax.ShapeDtypeStruct((m, d), BF16),
        compiler_params=_params(("arbitrary",)),
        name="norm_mod",
    )(x2, gain3, mod3, mod3)


def _proj_kernel(plan, x_ref, w_ref, cos_ref, sin_ref, gain_ref, *out_refs):
    x = x_ref[...]
    cos = cos_ref[...]
    sin = sin_ref[...]
    sub = 4
    for c0 in range(0, len(plan), sub):
        c1 = min(c0 + sub, len(plan))
        acc = _dot(x, w_ref[:, c0 * LANE:c1 * LANE])
        for c in range(c0, c1):
            kind, gi, post, dests = plan[c]
            y = acc[:, (c - c0) * LANE:(c - c0 + 1) * LANE]
            if kind == "nr":
                y = _rope(_rms(y, gain_ref[gi:gi + 1, :]), cos, sin)
                if post != 1.0:
                    y = y * post
            elif kind == "sig":
                y = jax.nn.sigmoid(y)
            for oi, oc in dests:
                out_refs[oi][:, oc * LANE:(oc + 1) * LANE] = y.astype(out_refs[oi].dtype)


def _proj(x, w, cos, sin, gains, plan, outs, rows, n_tiles=1):
    m, k = x.shape
    tn = len(plan) * LANE
    return pl.pallas_call(
        functools.partial(_proj_kernel, tuple(plan)),
        grid=(n_tiles, m // rows.tm),
        in_specs=[pl.BlockSpec((rows.tm, k), lambda j, i: (i, 0)),
                  pl.BlockSpec((k, tn), lambda j, i: (0, j)),
                  pl.BlockSpec((rows.tm, LANE), lambda j, i: (i, 0)),
                  pl.BlockSpec((rows.tm, LANE), lambda j, i: (i, 0)),
                  pl.BlockSpec(gains.shape, lambda j, i: (0, 0))],
        out_specs=[pl.BlockSpec((rows.tm, nc * LANE), lambda j, i: (i, j)) for nc, _ in outs],
        out_shape=[jax.ShapeDtypeStruct((m, n_tiles * nc * LANE), dt) for nc, dt in outs],
        compiler_params=_params(("arbitrary", "arbitrary")),
        name="in_projection",
    )(x, w, cos, sin, gains)


def _softmax_terms(s_list, m_list, guard):
    sm = [jnp.where(m, s, NEG) for s, m in zip(s_list, m_list)]
    mx = jnp.max(sm[0], axis=-1, keepdims=True)
    for x in sm[1:]:
        mx = jnp.maximum(mx, jnp.max(x, axis=-1, keepdims=True))
    es = [jnp.exp(x - mx) for x in sm]
    if guard:
        es = [jnp.where(m, e, 0.0) for e, m in zip(es, m_list)]
    tot = jnp.sum(es[0], axis=-1, keepdims=True)
    for e in es[1:]:
        tot = tot + jnp.sum(e, axis=-1, keepdims=True)
    inv = jnp.where(tot > 0.0, 1.0 / tot, 0.0) if guard else 1.0 / tot
    return es, inv


def _attend(q, pieces):
    s_list = [_dot_t(q, k) for k, _, _ in pieces]
    es, inv = _softmax_terms(s_list, [m for _, _, m in pieces], guard=False)
    o = _dot(es[0].astype(BF16), pieces[0][1])
    for e, (_, v, _) in zip(es[1:], pieces[1:]):
        o = o + _dot(e.astype(BF16), v)
    return o * inv


def _compress(z, wphi_ref, gk_ref, cosc_ref, sinc_ref, ck_ref, cv_ref):
    w0 = wphi_ref[0].astype(BF16)
    w1 = wphi_ref[1].astype(BF16)
    for g in range(NSA_KV_HEADS):
        zk = z[:, g * LANE:(g + 1) * LANE].astype(BF16)
        zv = z[:, (NSA_KV_HEADS + g) * LANE:(NSA_KV_HEADS + g + 1) * LANE].astype(BF16)
        ck = _rope(_rms(_dot(zk, w0), gk_ref[0:1, :]), cosc_ref[...], sinc_ref[...])
        ck_ref[g] = ck.astype(BF16)
        cv_ref[g] = _dot(zv, w1).astype(BF16)


def _pool(rows_f32, pe_t):
    n = rows_f32.shape[0] // CMP_BLOCK
    x3 = rows_f32.reshape(n, CMP_BLOCK, rows_f32.shape[1]) * pe_t[None]
    return jnp.sum(x3, axis=1) * (1.0 / CMP_BLOCK)


def _select_blocks(p_c, t1, q_rows, n_sel_blocks):
    qn = q_rows
    imp = p_c[0:qn] + p_c[qn:2 * qn] + p_c[2 * qn:3 * qn] + p_c[3 * qn:4 * qn]
    pair = imp + pltpu.roll(imp, LANE - 1, 1)
    lane = lax.broadcasted_iota(jnp.int32, (qn, LANE), 1)
    blk = lane >> 1
    lane_ok = ((lane & 1) == 0) & (lane < 2 * n_sel_blocks)
    cur = t1 >> 6
    forced = (blk == 0) | (blk == cur) | (blk == cur - 1)
    valid = lane_ok & ((blk << 6) <= t1)
    score = jnp.where(valid, pair + jnp.where(forced, FORCE, 0.0), -1.0)
    rank = jnp.zeros((qn, LANE), F32)
    for b in range(n_sel_blocks):
        col = score[:, 2 * b:2 * b + 1]
        beats = (col > score) | ((col == score) & (lane > 2 * b))
        rank = rank + jnp.where(beats, 1.0, 0.0)
    sel = (rank < float(min(SEL_TOPK, n_sel_blocks))) & (score >= 0.0)
    return jnp.where(sel, 1.0, 0.0)


def _nsa_group(qg, t1, ck, cv, n_cmp, n_sel_blocks, sel_pieces, win_pieces, e_mat):
    qn = t1.shape[0]
    t4 = jnp.concatenate([t1] * NSA_GROUP, axis=0)
    s = _dot_t(qg, ck)
    lane = lax.broadcasted_iota(jnp.int32, (1, NCMP_PAD), 1)
    cpos = jnp.where(lane < n_cmp, lane * CMP_BLOCK + (CMP_BLOCK - 1), jnp.int32(2 ** 30))
    es, inv = _softmax_terms([s], [cpos <= t4], guard=True)
    p_c = es[0] * inv
    o_c = _dot(p_c.astype(BF16), cv)
    sel = _select_blocks(p_c, t1, qn, n_sel_blocks)
    sel4 = jnp.concatenate([sel] * NSA_GROUP, axis=0)
    pieces = []
    for k, v, kpos, block_lane in sel_pieces:
        if block_lane is None:
            flag = _dot(sel4.astype(BF16), e_mat)
        else:
            flag = sel4[:, block_lane:block_lane + 1]
        pieces.append((k, v, jnp.where(kpos <= t4, flag, 0.0) > 0.5))
    o_s = _attend(qg, pieces)
    pieces = []
    for k, v, kpos in win_pieces:
        dist = t4 - kpos
        pieces.append((k, v, (dist >= 0) & (dist < WINDOW) & (kpos >= 0)))
    o_w = _attend(qg, pieces)
    return o_c, o_s, o_w


def _nsa_write(o_ref, gates, g, qn, o_c, o_s, o_w):
    for h in range(NSA_GROUP):
        hh = g * NSA_GROUP + h
        r = slice(h * qn, (h + 1) * qn)
        o = (gates[:, 3 * hh:3 * hh + 1] * o_c[r] + gates[:, 3 * hh + 1:3 * hh + 2] * o_s[r]
             + gates[:, 3 * hh + 2:3 * hh + 3] * o_w[r])
        o_ref[:, hh * LANE:(hh + 1) * LANE] = o.astype(o_ref.dtype)


def _lambda(lam_ref, lam_init):
    lv = lam_ref[...]
    return (jnp.exp(jnp.sum(lv[0:1] * lv[1:2], axis=-1, keepdims=True))
            - jnp.exp(jnp.sum(lv[2:3] * lv[3:4], axis=-1, keepdims=True)) + lam_init)


def _diff_head(q1, q2, pieces, lam, gain, lam_init):
    masks = [m for _, _, _, m in pieces]
    e1, inv1 = _softmax_terms([_dot_t(q1, k1) for k1, _, _, _ in pieces], masks, guard=False)
    e2, inv2 = _softmax_terms([_dot_t(q2, k2) for _, k2, _, _ in pieces], masks, guard=False)
    o = None
    for a1, a2, (_, _, v, _) in zip(e1, e2, pieces):
        a = a1 * inv1 - lam * (a2 * inv2)
        term = _dot(a.astype(BF16), v)
        o = term if o is None else o + term
    return _rms(o, gain) * (1.0 - lam_init)


Q_NSA = 128
Q_DIFF = 256
WIN_KEYS = WINDOW + Q_NSA


def _nsa_prompt_kernel(seq, q_ref, g_ref, cmp_ref, kv_ref, pe_ref, wphi_ref, gk_ref, cosc_ref, sinc_ref,
                       e_ref, o_ref, ck_ref, cv_ref):
    i = pl.program_id(1)

    @pl.when(i == 0)
    def _():
        z = _pool(cmp_ref[...], pe_ref[...])
        z = jnp.concatenate([z, jnp.zeros((NCMP_PAD - z.shape[0], z.shape[1]), F32)], axis=0)
        _compress(z, wphi_ref, gk_ref, cosc_ref, sinc_ref, ck_ref, cv_ref)

    t0 = i * Q_NSA
    t1 = t0 + lax.broadcasted_iota(jnp.int32, (Q_NSA, 1), 0)
    kpos = lax.broadcasted_iota(jnp.int32, (1, seq), 1)
    start = pl.multiple_of(jnp.clip(t0 - WINDOW, 0, seq - WIN_KEYS), Q_NSA)
    wpos = start + lax.broadcasted_iota(jnp.int32, (1, WIN_KEYS), 1)
    gates = g_ref[...]
    for g in range(NSA_KV_HEADS):
        qg = jnp.concatenate([q_ref[:, (g * NSA_GROUP + h) * LANE:(g * NSA_GROUP + h + 1) * LANE]
                              for h in range(NSA_GROUP)], axis=0)
        sel_pieces = [(kv_ref[:, g * LANE:(g + 1) * LANE], kv_ref[:, (2 + g) * LANE:(3 + g) * LANE], kpos, None)]
        win_pieces = [(kv_ref[pl.ds(start, WIN_KEYS), (4 + g) * LANE:(5 + g) * LANE],
                       kv_ref[pl.ds(start, WIN_KEYS), (6 + g) * LANE:(7 + g) * LANE], wpos)]
        o_c, o_s, o_w = _nsa_group(qg, t1, ck_ref[g], cv_ref[g], seq // CMP_BLOCK, seq // SEL_BLOCK,
                                   sel_pieces, win_pieces, e_ref[...])
        _nsa_write(o_ref, gates, g, Q_NSA, o_c, o_s, o_w)


def _nsa_prompt(q, gates, nsa_rows, kv_bf, pe_t, w_phi, g_nsa_k, cosc, sinc, e_mat, layer, batch, seq):
    m = q.shape[0]
    nq = seq // Q_NSA
    gate_blk = gates.shape[1] // LANE - 1
    return pl.pallas_call(
        functools.partial(_nsa_prompt_kernel, seq),
        grid=(batch, nq),
        in_specs=[pl.BlockSpec((Q_NSA, NSA_HEADS * HEAD_DIM), lambda b, i: (b * nq + i, 0)),
                  pl.BlockSpec((Q_NSA, LANE), lambda b, i: (b * nq + i, gate_blk)),
                  pl.BlockSpec((seq, 4 * LANE), lambda b, i: (b, 0)),
                  pl.BlockSpec((seq, 8 * LANE), lambda b, i: (b, 0)),
                  pl.BlockSpec((None, CMP_BLOCK, 4 * LANE), lambda b, i: (layer, 0, 0)),
                  pl.BlockSpec((None, 2, HEAD_DIM, HEAD_DIM), lambda b, i: (layer, 0, 0, 0)),
                  pl.BlockSpec((None, 3, HEAD_DIM), lambda b, i: (layer, 0, 0)),
                  pl.BlockSpec((NCMP_PAD, LANE), lambda b, i: (0, 0)),
                  pl.BlockSpec((NCMP_PAD, LANE), lambda b, i: (0, 0)),
                  pl.BlockSpec(e_mat.shape, lambda b, i: (0, 0))],
        out_specs=pl.BlockSpec((Q_NSA, NSA_HEADS * HEAD_DIM), lambda b, i: (b * nq + i, 0)),
        out_shape=jax.ShapeDtypeStruct((m, NSA_HEADS * HEAD_DIM), BF16),
        scratch_shapes=[pltpu.VMEM((NSA_KV_HEADS, NCMP_PAD, HEAD_DIM), BF16),
                        pltpu.VMEM((NSA_KV_HEADS, NCMP_PAD, HEAD_DIM), BF16)],
        compiler_params=_params(("arbitrary", "arbitrary")),
        name="nsa_prompt",
    )(q, gates, nsa_rows, kv_bf, pe_t, w_phi, g_nsa_k, cosc, sinc, e_mat)


def _diff_prompt_kernel(seq, lam_init, q_ref, kv_ref, lam_ref, gd_ref, o_ref):
    i = pl.program_id(1)
    t1 = i * Q_DIFF + lax.broadcasted_iota(jnp.int32, (Q_DIFF, 1), 0)
    mask = lax.broadcasted_iota(jnp.int32, (1, seq), 1) <= t1
    lam = _lambda(lam_ref, lam_init)
    for h in range(DIFF_HEADS):
        c = 2 * h * LANE
        pieces = [(kv_ref[:, c:c + LANE], kv_ref[:, c + LANE:c + 2 * LANE],
                   kv_ref[:, DIFF_HEADS * DIFF_V + h * DIFF_V:DIFF_HEADS * DIFF_V + (h + 1) * DIFF_V], mask)]
        o = _diff_head(q_ref[:, c:c + LANE], q_ref[:, c + LANE:c + 2 * LANE], pieces, lam, gd_ref[...], lam_init)
        o_ref[:, h * DIFF_V:(h + 1) * DIFF_V] = o.astype(o_ref.dtype)


def _diff_prompt(q, kv_bf, lam_diff, g_dout3, layer, lam_init, batch, seq):
    m = q.shape[0]
    nq = seq // Q_DIFF
    width = DIFF_HEADS * DIFF_V
    return pl.pallas_call(
        functools.partial(_diff_prompt_kernel, seq, lam_init),
        grid=(batch, nq),
        in_specs=[pl.BlockSpec((Q_DIFF, width), lambda b, i: (b * nq + i, 0)),
                  pl.BlockSpec((seq, 2 * width), lambda b, i: (b, 0)),
                  pl.BlockSpec((None, 4, DIFF_QK), lambda b, i: (layer, 0, 0)),
                  pl.BlockSpec((None, 1, DIFF_V), lambda b, i: (layer, 0, 0))],
        out_specs=pl.BlockSpec((Q_DIFF, width), lambda b, i: (b * nq + i, 0)),
        out_shape=jax.ShapeDtypeStruct((m, width), BF16),
        compiler_params=_params(("arbitrary", "arbitrary")),
        name="diff_prompt",
    )(q, kv_bf, lam_diff, g_dout3)


def _pad_rows(x, n):
    return jnp.concatenate([x, jnp.zeros((n - x.shape[0], x.shape[1]), x.dtype)], axis=0)


def _nsa_sample_kernel(n_pages, page, steps, pt_ref, *refs):
    pages = refs[:n_pages]
    (q_ref, g_ref, new_ref, wnew_ref, wbuf_ref, pe_ref, wphi_ref, gk_ref, cosc_ref, sinc_ref, e_ref,
     o_ref, kbuf, zbuf, ck_ref, cv_ref) = refs[n_pages:]
    past = n_pages * page
    per_page = page // CMP_BLOCK
    pe_t = pe_ref[...]
    zbuf[...] = jnp.zeros(zbuf.shape, F32)
    for p in range(n_pages):
        for c in range(4):
            cs = slice(c * LANE, (c + 1) * LANE)
            zbuf[p * per_page:(p + 1) * per_page, cs] = _pool(pages[p][pl.ds(c, page, stride=8), :], pe_t[:, cs])
            kbuf[p * page:(p + 1) * page, cs] = pages[p][pl.ds(4 + c, page, stride=8), :].astype(BF16)
    new = new_ref[...]
    z_new = jnp.sum(new[:, 0:4 * LANE] * pe_t[0:steps, :], axis=0, keepdims=True) * (1.0 / CMP_BLOCK)
    zbuf[past // CMP_BLOCK:past // CMP_BLOCK + 1, :] = z_new
    _compress(zbuf[...], wphi_ref, gk_ref, cosc_ref, sinc_ref, ck_ref, cv_ref)

    padded = -(-(past + steps) // SEL_BLOCK) * SEL_BLOCK
    n_cmp = padded // CMP_BLOCK
    n_sel = padded // SEL_BLOCK
    t1 = past + lax.broadcasted_iota(jnp.int32, (steps, 1), 0)
    kpos_past = lax.broadcasted_iota(jnp.int32, (1, past), 1)
    kpos_new = past + lax.broadcasted_iota(jnp.int32, (1, LANE), 1)
    kpos_new = jnp.where(kpos_new < past + steps, kpos_new, jnp.int32(2 ** 30))
    w_buf = wbuf_ref.shape[0] // 4
    wpos_past = (past - w_buf) + lax.broadcasted_iota(jnp.int32, (1, w_buf), 1)
    gates = g_ref[...]
    wnew = wnew_ref[...]
    q = q_ref[...]
    for g in range(NSA_KV_HEADS):
        qg = jnp.concatenate([q[:, (g * NSA_GROUP + h) * LANE:(g * NSA_GROUP + h + 1) * LANE]
                              for h in range(NSA_GROUP)], axis=0).astype(BF16)
        k_new = _pad_rows(new[:, (4 + g) * LANE:(5 + g) * LANE], LANE).astype(BF16)
        v_new = _pad_rows(new[:, (6 + g) * LANE:(7 + g) * LANE], LANE).astype(BF16)
        sel_pieces = [(kbuf[:, g * LANE:(g + 1) * LANE], kbuf[:, (2 + g) * LANE:(3 + g) * LANE], kpos_past, None),
                      (k_new, v_new, kpos_new, 2 * (past // SEL_BLOCK))]
        wk_new = _pad_rows(wnew[:, g * LANE:(g + 1) * LANE], LANE).astype(BF16)
        wv_new = _pad_rows(wnew[:, (2 + g) * LANE:(3 + g) * LANE], LANE).astype(BF16)
        win_pieces = [(wbuf_ref[pl.ds(g, w_buf, stride=4), :].astype(BF16),
                       wbuf_ref[pl.ds(2 + g, w_buf, stride=4), :].astype(BF16), wpos_past),
                      (wk_new, wv_new, kpos_new)]
        o_c, o_s, o_w = _nsa_group(qg, t1, ck_ref[g], cv_ref[g], n_cmp, n_sel, sel_pieces, win_pieces, e_ref[...])
        _nsa_write(o_ref, gates, g, steps, o_c, o_s, o_w)


def _nsa_sample(page_table, cache4, q3, gates3, new3, wnew3, win_state, pe_t, w_phi, g_nsa_k, cosc, sinc,
                e_mat, layer):
    n_seq, n_pages = page_table.shape
    page = cache4.shape[2] // 8
    steps = q3.shape[1]
    past = n_pages * page
    gate_blk = gates3.shape[2] // LANE - 1
    w_rows = win_state.shape[2]

    def page_spec(p):
        return pl.BlockSpec((None, None, 8 * page, LANE), lambda s, pt: (layer, pt[s, p], 0, 0))

    def seq_spec(width, blk=0):
        return pl.BlockSpec((None, steps, width), lambda s, pt: (s, 0, blk))

    def const_spec(shape):
        return pl.BlockSpec(shape, lambda s, pt: (0,) * len(shape))

    grid_spec = pltpu.PrefetchScalarGridSpec(
        num_scalar_prefetch=1,
        grid=(n_seq,),
        in_specs=[page_spec(p) for p in range(n_pages)] + [
            seq_spec(NSA_HEADS * HEAD_DIM), seq_spec(LANE, gate_blk), seq_spec(8 * LANE), seq_spec(4 * LANE),
            pl.BlockSpec((None, None, w_rows, LANE), lambda s, pt: (layer, s, 0, 0)),
            pl.BlockSpec((None, CMP_BLOCK, 4 * LANE), lambda s, pt: (layer, 0, 0)),
            pl.BlockSpec((None, 2, HEAD_DIM, HEAD_DIM), lambda s, pt: (layer, 0, 0, 0)),
            pl.BlockSpec((None, 3, HEAD_DIM), lambda s, pt: (layer, 0, 0)),
            const_spec((NCMP_PAD, LANE)), const_spec((NCMP_PAD, LANE)), const_spec(e_mat.shape)],
        out_specs=pl.BlockSpec((None, steps, NSA_HEADS * HEAD_DIM), lambda s, pt: (s, 0, 0)),
        scratch_shapes=[pltpu.VMEM((past, 4 * LANE), BF16),
                        pltpu.VMEM((NCMP_PAD, 4 * LANE), F32),
                        pltpu.VMEM((NSA_KV_HEADS, NCMP_PAD, HEAD_DIM), BF16),
                        pltpu.VMEM((NSA_KV_HEADS, NCMP_PAD, HEAD_DIM), BF16)])
    return pl.pallas_call(
        functools.partial(_nsa_sample_kernel, n_pages, page, steps),
        grid_spec=grid_spec,
        out_shape=jax.ShapeDtypeStruct((n_seq, steps, NSA_HEADS * HEAD_DIM), F32),
        compiler_params=_params(("arbitrary",)),
        name="nsa_sample",
    )(page_table, *([cache4] * n_pages), q3, gates3, new3, wnew3, win_state, pe_t, w_phi, g_nsa_k, cosc, sinc,
      e_mat)


def _diff_sample_kernel(n_pages, page, steps, lam_init, pt_ref, *refs):
    pages = refs[:n_pages]
    q_ref, new_ref, lam_ref, gd_ref, o_ref, kvbuf = refs[n_pages:]
    past = n_pages * page
    for p in range(n_pages):
        for c in range(2 * DIFF_HEADS * 2):
            slot, head, half = c // 8, (c // 2) % DIFF_HEADS, c % 2
            kvbuf[p * page:(p + 1) * page, c * LANE:(c + 1) * LANE] = pages[p][
                pl.ds(8 * slot + 4 * half + head, page, stride=16), :].astype(BF16)
    t1 = past + lax.broadcasted_iota(jnp.int32, (steps, 1), 0)
    mask_past = lax.broadcasted_iota(jnp.int32, (1, past), 1) <= t1
    mask_new = (past + lax.broadcasted_iota(jnp.int32, (1, LANE), 1)) <= t1
    lam = _lambda(lam_ref, lam_init)
    new = new_ref[...]
    q = q_ref[...].astype(BF16)
    vbase = DIFF_HEADS * DIFF_V
    for h in range(DIFF_HEADS):
        c = 2 * h * LANE
        k1n = _pad_rows(new[:, c:c + LANE], LANE).astype(BF16)
        k2n = _pad_rows(new[:, c + LANE:c + 2 * LANE], LANE).astype(BF16)
        vn = _pad_rows(new[:, vbase + h * DIFF_V:vbase + (h + 1) * DIFF_V], LANE).astype(BF16)
        pieces = [(kvbuf[:, c:c + LANE], kvbuf[:, c + LANE:c + 2 * LANE],
                   kvbuf[:, vbase + h * DIFF_V:vbase + (h + 1) * DIFF_V], mask_past),
                  (k1n, k2n, vn, mask_new)]
        o = _diff_head(q[:, c:c + LANE], q[:, c + LANE:c + 2 * LANE], pieces, lam, gd_ref[...], lam_init)
        o_ref[:, h * DIFF_V:(h + 1) * DIFF_V] = o.astype(o_ref.dtype)


def _diff_sample(page_table, cache4, q3, new3, lam_diff, g_dout3, layer, lam_init):
    n_seq, n_pages = page_table.shape
    page = cache4.shape[2] // 16
    steps = q3.shape[1]
    width = DIFF_HEADS * DIFF_V
    grid_spec = pltpu.PrefetchScalarGridSpec(
        num_scalar_prefetch=1,
        grid=(n_seq,),
        in_specs=[pl.BlockSpec((None, None, 16 * page, LANE), functools.partial(
            lambda p, s, pt: (layer, pt[s, p], 0, 0), p)) for p in range(n_pages)] + [
            pl.BlockSpec((None, steps, width), lambda s, pt: (s, 0, 0)),
            pl.BlockSpec((None, steps, 2 * width), lambda s, pt: (s, 0, 0)),
            pl.BlockSpec((None, 4, DIFF_QK), lambda s, pt: (layer, 0, 0)),
            pl.BlockSpec((None, 1, DIFF_V), lambda s, pt: (layer, 0, 0))],
        out_specs=pl.BlockSpec((None, steps, width), lambda s, pt: (s, 0, 0)),
        scratch_shapes=[pltpu.VMEM((n_pages * page, 2 * width), BF16)])
    return pl.pallas_call(
        functools.partial(_diff_sample_kernel, n_pages, page, steps, lam_init),
        grid_spec=grid_spec,
        out_shape=jax.ShapeDtypeStruct((n_seq, steps, width), F32),
        compiler_params=_params(("arbitrary",)),
        name="diff_sample",
    )(page_table, *([cache4] * n_pages), q3, new3, lam_diff, g_dout3)


def _merge_kernel(a_ref, b_ref, wa_ref, wb_ref, ga_ref, gb_ref, o_ref, wa_s, wb_s):
    @pl.when(pl.program_id(1) == 0)
    def _():
        wa_s[...] = wa_ref[...].astype(BF16)
        wb_s[...] = wb_ref[...].astype(BF16)

    oa = _dot(a_ref[...].astype(BF16), wa_s[...])
    ob = _dot(b_ref[...].astype(BF16), wb_s[...])
    o_ref[...] = (ga_ref[...] * oa + gb_ref[...] * ob).astype(o_ref.dtype)


def _merge(attn_a, attn_b, w_branch, gates, layer, rows):
    m, ka = attn_a.shape
    kb = attn_b.shape[1]
    d = w_branch.shape[3]
    tn = 1024
    nj = d // tn
    return pl.pallas_call(
        _merge_kernel,
        grid=(nj, m // rows.tm),
        in_specs=[pl.BlockSpec((rows.tm, ka), lambda j, i: (i, 0)),
                  pl.BlockSpec((rows.tm, kb), lambda j, i: (i, 0)),
                  pl.BlockSpec((None, None, ka, tn), lambda j, i: (layer, 0, 0, j)),
                  pl.BlockSpec((None, None, kb, tn), lambda j, i: (layer, 1, 0, j)),
                  pl.BlockSpec((rows.tm, tn), lambda j, i: (i, j)),
                  pl.BlockSpec((rows.tm, tn), lambda j, i: (i, nj + j))],
        out_specs=pl.BlockSpec((rows.tm, tn), lambda j, i: (i, j)),
        out_shape=jax.ShapeDtypeStruct((m, d), BF16),
        scratch_shapes=[pltpu.VMEM((ka, tn), BF16), pltpu.VMEM((kb, tn), BF16)],
        compiler_params=_params(("arbitrary", "arbitrary")),
        name="merge_branches",
    )(attn_a, attn_b, w_branch, w_branch, gates, gates)


def _resid_kernel(rows, cast_w, a_ref, w_ref, x_ref, gt_ref, o_ref, *scratch):
    if cast_w:
        w_s, = scratch

        @pl.when(pl.program_id(1) == 0)
        def _():
            w_s[...] = w_ref[...].astype(BF16)

        w = w_s[...]
    else:
        w = w_ref[...]
    gt = _rows_bcast(gt_ref[...], rows, o_ref.shape[1])
    o_ref[...] = x_ref[...] + gt * _dot(a_ref[...], w)


def _resid_proj(a, w, w_layer, x2, mod3, comp_blocks, rows, tn):
    m, k = a.shape
    n = x2.shape[1]
    cast_w = w_layer is not None
    if cast_w:
        w_spec = pl.BlockSpec((None, k, tn), lambda j, i: (w_layer, 0, j))
        scratch = [pltpu.VMEM((k, tn), BF16)]
    else:
        w_spec = pl.BlockSpec((k, tn), lambda j, i: (0, j))
        scratch = []
    off = comp_blocks * (n // tn)
    return pl.pallas_call(
        functools.partial(_resid_kernel, rows, cast_w),
        grid=(n // tn, m // rows.tm),
        in_specs=[pl.BlockSpec((rows.tm, k), lambda j, i: (i, 0)),
                  w_spec,
                  pl.BlockSpec((rows.tm, tn), lambda j, i: (i, j)),
                  pl.BlockSpec((rows.bb, 1, tn), lambda j, i: (i // rows.mod_div, 0, off + j))],
        out_specs=pl.BlockSpec((rows.tm, tn), lambda j, i: (i, j)),
        out_shape=jax.ShapeDtypeStruct((m, n), F32),
        scratch_shapes=scratch,
        compiler_params=_params(("arbitrary", "arbitrary")),
        name="residual_projection",
    )(a, w, x2, mod3)


def _ffn_up_kernel(rows, carry_prev, h_ref, wg_ref, wv_ref, cw_ref, cb_ref, prev_ref, act_ref, st_ref, *scratch):
    tn = act_ref.shape[1]
    h = h_ref[...]
    ug = _dot(h, wg_ref[...])
    uv = _dot(h, wv_ref[...])
    ug3 = ug.reshape(rows.bb, rows.tt, tn)
    if carry_prev:
        carry, = scratch

        @pl.when(pl.program_id(1) % rows.mod_div == 0)
        def _():
            carry[...] = prev_ref[...]

        prev = carry[...]
    else:
        prev = prev_ref[...]
    p0 = prev[:, 0:1, :]
    p1 = prev[:, 1:2, :]
    rid = lax.broadcasted_iota(jnp.int32, ug3.shape, 1)
    u1 = jnp.where(rid == 0, p1, pltpu.roll(ug3, 1, 1))
    u2 = jnp.where(rid == 0, p0, jnp.where(rid == 1, p1, pltpu.roll(ug3, 2, 1)))
    cw = cw_ref[...]
    conv = cb_ref[...] + u2 * cw[0:1, :] + u1 * cw[1:2, :] + ug3 * cw[2:3, :]
    act = jax.nn.gelu(conv).reshape(rows.tm, tn) * uv
    act_ref[...] = act.astype(act_ref.dtype)
    tail = ug3[:, rows.tt - (CONV_W - 1):rows.tt, :]
    st_ref[...] = tail
    if carry_prev:
        carry[...] = tail


def _ffn_up(h, w_up_p, conv_w_p, conv_b_p, prev, rows, carry_prev):
    m, k = h.shape
    fp = w_up_p.shape[1] // 2
    tn = TN_FF
    nj = fp // tn
    n_seq = prev.shape[0]
    scratch = [pltpu.VMEM((1, CONV_W - 1, tn), F32)] if carry_prev else []
    return pl.pallas_call(
        functools.partial(_ffn_up_kernel, rows, carry_prev),
        grid=(nj, m // rows.tm),
        in_specs=[pl.BlockSpec((rows.tm, k), lambda j, i: (i, 0)),
                  pl.BlockSpec((k, tn), lambda j, i: (0, j)),
                  pl.BlockSpec((k, tn), lambda j, i: (0, nj + j)),
                  pl.BlockSpec((CONV_W, tn), lambda j, i: (0, j)),
                  pl.BlockSpec((1, tn), lambda j, i: (0, j)),
                  pl.BlockSpec((rows.bb, CONV_W - 1, tn), lambda j, i: (i // rows.mod_div, 0, j))],
        out_specs=[pl.BlockSpec((rows.tm, tn), lambda j, i: (i, j)),
                   pl.BlockSpec((rows.bb, CONV_W - 1, tn), lambda j, i: (i // rows.mod_div, 0, j))],
        out_shape=[jax.ShapeDtypeStruct((m, fp), BF16),
                   jax.ShapeDtypeStruct((n_seq, CONV_W - 1, fp), F32)],
        scratch_shapes=scratch,
        compiler_params=_params(("arbitrary", "arbitrary")),
        name="ffn_up_conv",
    )(h, w_up_p, w_up_p, conv_w_p, conv_b_p, prev)


def _rope_tables(pos):
    half = HEAD_DIM // 2
    freqs = ROPE_THETA ** (-jnp.arange(half, dtype=F32) / half)
    ang = pos.astype(F32)[:, None] * freqs
    cos, sin = jnp.cos(ang), jnp.sin(ang)
    return jnp.concatenate([cos, cos], axis=-1), jnp.concatenate([-sin, sin], axis=-1)


def _proj_plans():
    q_scale = HEAD_DIM ** -0.5
    d_scale = DIFF_QK ** -0.5
    plan_a = [("nr", 0, q_scale, ((0, c),)) for c in range(8)]
    plan_a += [("id", 0, 1.0, ((1, c),)) for c in range(4)]
    plan_a += [("nr", 3, 1.0, ((1, 4 + c), (3, c))) for c in range(2)]
    plan_a += [("id", 0, 1.0, ((1, 6 + c), (3, 2 + c))) for c in range(2)]
    plan_a += [("nr", 4, 1.0, ((2, c), (3, 4 + c))) for c in range(2)]
    plan_a += [("id", 0, 1.0, ((2, 2 + c), (3, 6 + c))) for c in range(2)]
    plan_b = [("nr", 5, d_scale, ((0, c),)) for c in range(8)]
    plan_b += [("nr", 6, 1.0, ((1, c), (2, c))) for c in range(8)]
    plan_b += [("id", 0, 1.0, ((1, 8 + c), (2, 8 + c))) for c in range(8)]
    return plan_a, plan_b


def kernel(x_prompt, x_sample, cache_nsa_kv, cache_diff_kv, state_win_kv, state_conv, page_table, c_prompt,
           c_sample, w_ada, b_ada, norm1_g, norm2_g, w_in, g_nsa_q, g_nsa_k, pe_cmp, w_phi, g_diff_q, g_diff_k,
           lam_diff, g_diff_out, w_branch, w_out, w_up, conv_w, conv_b, w_down):
    batch, seq, d = x_prompt.shape
    n_seq, steps, _ = x_sample.shape
    depth = w_in.shape[0]
    n_pages, page = page_table.shape[1], cache_nsa_kv.shape[2]
    past = n_pages * page
    n_pool = cache_nsa_kv.shape[1]
    d_ff = w_down.shape[1]
    fp = -(-d_ff // TN_FF) * TN_FF

    rows_p = Rows(batch * seq, TM, 1, TM, seq // TM)
    tm_s = min(TM, n_seq * steps)
    rows_s = Rows(n_seq * steps, tm_s, tm_s // steps, steps, 1)

    pos_p = jnp.tile(jnp.arange(seq), batch)
    pos_s = jnp.tile(past + jnp.arange(steps), n_seq)
    cos_p, sin_p = _rope_tables(pos_p)
    cos_s, sin_s = _rope_tables(pos_s)
    cosc, sinc = _rope_tables(jnp.arange(NCMP_PAD) * CMP_BLOCK + (CMP_BLOCK - 1))
    e_rows = jnp.arange(LANE)[:, None]
    e_p = ((e_rows % 2 == 0) & (e_rows // 2 == jnp.arange(seq)[None, :] // SEL_BLOCK)).astype(BF16)
    e_s = ((e_rows % 2 == 0) & (e_rows // 2 == jnp.arange(past)[None, :] // SEL_BLOCK)).astype(BF16)

    mb = -(-(batch + n_seq) // 8) * 8
    c_all = jnp.pad(jnp.concatenate([c_prompt, c_sample], axis=0), ((0, mb - batch - n_seq), (0, 0)))
    b_ada3 = b_ada[:, None, :]
    norm1_3 = norm1_g[:, None, :]
    norm2_3 = norm2_g[:, None, :]
    g_dout3 = g_diff_out[:, None, :]
    pe_t = jnp.concatenate([pe_cmp[:, 0], pe_cmp[:, 0], pe_cmp[:, 1], pe_cmp[:, 1]], axis=-1)
    cache_nsa4 = cache_nsa_kv.reshape(depth, n_pool, page * 8, LANE)
    cache_diff4 = cache_diff_kv.reshape(depth, n_pool, page, 2, DIFF_HEADS, 2, LANE).transpose(
        0, 1, 2, 3, 5, 4, 6).reshape(depth, n_pool, page * 16, LANE)
    win_state4 = state_win_kv.reshape(depth, n_seq, state_win_kv.shape[2] * 4, LANE)
    plan_a, plan_b = _proj_plans()
    plan_g = [("sig", 0, 1.0, ((0, c),)) for c in range(11)]

    n_q = NSA_HEADS * HEAD_DIM
    n_kv = 6 * NSA_KV_HEADS * HEAD_DIM
    n_g = 3 * NSA_HEADS
    n_qd = DIFF_HEADS * 2 * DIFF_QK
    o_g = n_q + n_kv
    o_d = o_g + n_g
    o_m = o_d + 3 * n_qd

    xp = x_prompt.reshape(batch * seq, d)
    xs = x_sample.reshape(n_seq * steps, d)
    outs = [[] for _ in range(8)]
    for l in range(depth):
        lam_init = 0.8 - 0.6 * math.exp(-0.3 * l)
        wl = w_in[l]
        w_a = wl[:, :o_g].astype(BF16)
        w_b = wl[:, o_d:o_m].astype(BF16)
        w_g = jnp.concatenate([wl[:, o_m:], wl[:, o_g:o_d],
                               jnp.zeros((d, LANE - n_g), F32)], axis=1).astype(BF16)
        gains = jnp.stack([g_nsa_q[l], g_nsa_q[l], g_nsa_q[l], g_nsa_k[l, 1], g_nsa_k[l, 2],
                           g_diff_q[l], g_diff_k[l], g_diff_k[l]])
        pad_f = ((0, 0), (0, fp - d_ff))
        w_up_p = jnp.concatenate([jnp.pad(w_up[l][:, :d_ff], pad_f), jnp.pad(w_up[l][:, d_ff:], pad_f)],
                                 axis=1).astype(BF16)
        w_down_p = jnp.pad(w_down[l], ((0, fp - d_ff), (0, 0))).astype(BF16)
        conv_w_p = jnp.pad(conv_w[l], pad_f)
        conv_b_p = jnp.pad(conv_b[l][None, :], pad_f)

        mod = _ada(c_all, w_ada, b_ada3, l)
        mod_p = mod[:batch, None, :]
        mod_s = mod[batch:batch + n_seq, None, :]

        new_x = []
        for grp, (x2, rows, mod3, cos, sin) in enumerate(((xp, rows_p, mod_p, cos_p, sin_p),
                                                          (xs, rows_s, mod_s, cos_s, sin_s))):
            sample = grp == 1
            qdt = F32 if sample else BF16
            h = _norm_mod(x2, norm1_3, mod3, l, 0, rows)
            q_nsa, nsa_rows, win_rows, kv_bf = _proj(
                h, w_a, cos, sin, gains, plan_a, [(8, qdt), (8, F32), (4, F32), (8, BF16)], rows)
            q_diff, diff_rows, dkv_bf = _proj(
                h, w_b, cos, sin, gains, plan_b, [(8, qdt), (16, F32), (16, BF16)], rows)
            gates, = _proj(h, w_g, cos, sin, gains, plan_g, [(11, F32)], rows, n_tiles=3)
            if not sample:
                a_nsa = _nsa_prompt(q_nsa, gates, nsa_rows, kv_bf, pe_t, w_phi, g_nsa_k, cosc, sinc, e_p,
                                    l, batch, seq)
                a_diff = _diff_prompt(q_diff, dkv_bf, lam_diff, g_dout3, l, lam_init, batch, seq)
            else:
                a_nsa = _nsa_sample(page_table, cache_nsa4, q_nsa.reshape(n_seq, steps, -1),
                                    gates.reshape(n_seq, steps, -1), nsa_rows.reshape(n_seq, steps, -1),
                                    win_rows.reshape(n_seq, steps, -1), win_state4, pe_t, w_phi, g_nsa_k,
                                    cosc, sinc, e_s, l).reshape(n_seq * steps, -1)
                a_diff = _diff_sample(page_table, cache_diff4, q_diff.reshape(n_seq, steps, -1),
                                      diff_rows.reshape(n_seq, steps, -1), lam_diff, g_dout3, l,
                                      lam_init).reshape(n_seq * steps, -1)
            merged = _merge(a_nsa, a_diff, w_branch, gates, l, rows)
            x_mid = _resid_proj(merged, w_out, l, x2, mod3, 2, rows, 1024)
            h2 = _norm_mod(x_mid, norm2_3, mod3, l, 3, rows)
            if not sample:
                prev = jnp.zeros((batch, CONV_W - 1, fp), F32)
            else:
                prev = jnp.pad(state_conv[l], ((0, 0), (0, 0), (0, fp - d_ff)))
            act, conv_state = _ffn_up(h2, w_up_p, conv_w_p, conv_b_p, prev, rows, carry_prev=not sample)
            x_new = _resid_proj(act, w_down_p, None, x_mid, mod3, 5, rows, 512)
            new_x.append(x_new)

            nb, nt = (n_seq, steps) if sample else (batch, seq)
            r_nsa = nsa_rows.reshape(nb, nt, 4, NSA_KV_HEADS, HEAD_DIM)
            r_diff = diff_rows.reshape(nb, nt, 2, DIFF_HEADS, DIFF_V)
            r_win = win_rows.reshape(nb, nt, 2, NSA_KV_HEADS, HEAD_DIM)
            if sample:
                r_win = jnp.concatenate([state_win_kv[l], r_win], axis=1)[:, steps:]
            else:
                r_win = r_win[:, nt - min(WINDOW, nt):]
            for slot, val in zip(range(4), (r_nsa, r_diff, r_win, conv_state[:, :, :d_ff])):
                outs[4 * grp + slot].append(val)
        xp, xs = new_x
    return (xp.reshape(batch, seq, d), xs.reshape(n_seq, steps, d)) + tuple(jnp.stack(o) for o in outs)
```

```python
import collections
import functools
import math

import jax
import jax.numpy as jnp
from jax import lax
from jax.experimental import pallas as pl
from jax.experimental.pallas import tpu as pltpu

F32 = jnp.float32
BF16 = jnp.bfloat16

HEAD_DIM = 128
NSA_HEADS = 8
NSA_KV_HEADS = 2
NSA_GROUP = NSA_HEADS // NSA_KV_HEADS
CMP_BLOCK = 32
SEL_BLOCK = 64
SEL_TOPK = 16
WINDOW = 512
DIFF_HEADS = 4
DIFF_QK = 128
DIFF_V = 2 * DIFF_QK
CONV_W = 3
ROPE_THETA = 10000.0
EPS = 1e-6
NEG = -1e30
FORCE = 100.0

LANE = 128
VMEM_LIMIT = 56 * 1024 * 1024
TM = 512
TM_FF = 1024
TN_FF = 512
NCMP_PAD = 128

Rows = collections.namedtuple("Rows", "m tm bb tt mod_div")


def _params(sem):
    return pltpu.CompilerParams(dimension_semantics=sem, vmem_limit_bytes=VMEM_LIMIT)


def _rows_bcast(v, rows, width):
    if rows.bb == 1:
        return v.reshape(1, width)
    return jnp.broadcast_to(v, (rows.bb, rows.tt, width)).reshape(rows.tm, width)


def _rms(y, gain):
    return y * lax.rsqrt(jnp.mean(y * y, axis=-1, keepdims=True) + EPS) * gain


def _rope(y, cos, sin_signed):
    return y * cos + pltpu.roll(y, HEAD_DIM // 2, 1) * sin_signed


def _dot(a, b):
    return jnp.dot(a, b, preferred_element_type=F32)


def _dot_t(a, b):
    return lax.dot_general(a, b, (((1,), (1,)), ((), ())), preferred_element_type=F32)


def _ada_kernel(c_ref, w_ref, b_ref, o_ref):
    c = c_ref[...]
    a = (c * jax.nn.sigmoid(c)).astype(BF16)
    o_ref[...] = _dot(a, w_ref[...].astype(BF16)) + b_ref[...]


def _ada(c_all, w_ada, b_ada3, layer):
    mb, d = c_all.shape
    n = w_ada.shape[2]
    tn = 1024
    return pl.pallas_call(
        _ada_kernel,
        grid=(n // tn,),
        in_specs=[pl.BlockSpec((mb, d), lambda j: (0, 0)),
                  pl.BlockSpec((None, d, tn), lambda j: (layer, 0, j)),
                  pl.BlockSpec((None, 1, tn), lambda j: (layer, 0, j))],
        out_specs=pl.BlockSpec((mb, tn), lambda j: (0, j)),
        out_shape=jax.ShapeDtypeStruct((mb, n), F32),
        compiler_params=_params(("arbitrary",)),
        name="ada_modulation",
    )(c_all, w_ada, b_ada3)


def _norm_mod_kernel(rows, x_ref, g_ref, sh_ref, sc_ref, o_ref):
    d = x_ref.shape[1]
    y = _rms(x_ref[...], g_ref[...])
    sc = _rows_bcast(sc_ref[...], rows, d)
    sh = _rows_bcast(sh_ref[...], rows, d)
    o_ref[...] = (y * (1.0 + sc) + sh).astype(o_ref.dtype)


def _norm_mod(x2, gain3, mod3, layer, comp, rows):
    m, d = x2.shape
    return pl.pallas_call(
        functools.partial(_norm_mod_kernel, rows),
        grid=(m // rows.tm,),
        in_specs=[pl.BlockSpec((rows.tm, d), lambda i: (i, 0)),
                  pl.BlockSpec((None, 1, d), lambda i: (layer, 0, 0)),
                  pl.BlockSpec((rows.bb, 1, d), lambda i: (i // rows.mod_div, 0, comp)),
                  pl.BlockSpec((rows.bb, 1, d), lambda i: (i // rows.mod_div, 0, comp + 1))],
        out_specs=pl.BlockSpec((rows.tm, d), lambda i: (i, 0)),
        out_shape=jax.ShapeDtypeStruct((m, d), BF16),
        compiler_params=_params(("arbitrary",)),
        name="norm_mod",
    )(x2, gain3, mod3, mod3)


def _proj_kernel(plan, per_token, x_ref, w_ref, cos_ref, sin_ref, gain_ref, *out_refs):
    x = x_ref[...]
    cos = cos_ref[...]
    sin = sin_ref[...]
    sub = 4
    for c0 in range(0, len(plan), sub):
        c1 = min(c0 + sub, len(plan))
        acc = _dot(x, w_ref[:, c0 * LANE:c1 * LANE])
        for c in range(c0, c1):
            kind, gi, post, dests = plan[c]
            y = acc[:, (c - c0) * LANE:(c - c0 + 1) * LANE]
            if kind == "nr":
                y = _rope(_rms(y, gain_ref[gi:gi + 1, :]), cos, sin)
                if post != 1.0:
                    y = y * post
            elif kind == "sig":
                y = jax.nn.sigmoid(y)
            for oi, oc in dests:
                ref = out_refs[oi]
                if per_token[oi]:
                    ref[pl.ds(oc, x.shape[0], stride=per_token[oi]), :] = y.astype(ref.dtype)
                else:
                    ref[:, oc * LANE:(oc + 1) * LANE] = y.astype(ref.dtype)


def _proj(x, w, cos, sin, gains, plan, outs, rows, n_tiles=1):
    m, k = x.shape
    tn = len(plan) * LANE
    out_specs, out_shape = [], []
    for layout, nc, dt in outs:
        if layout == "rows":
            assert n_tiles == 1
            out_specs.append(pl.BlockSpec((rows.tm * nc, LANE), lambda j, i: (i, 0)))
            out_shape.append(jax.ShapeDtypeStruct((m * nc, LANE), dt))
        else:
            out_specs.append(pl.BlockSpec((rows.tm, nc * LANE), lambda j, i: (i, j)))
            out_shape.append(jax.ShapeDtypeStruct((m, n_tiles * nc * LANE), dt))
    return pl.pallas_call(
        functools.partial(_proj_kernel, tuple(plan), tuple(nc if lay == "rows" else 0 for lay, nc, _ in outs)),
        grid=(n_tiles, m // rows.tm),
        in_specs=[pl.BlockSpec((rows.tm, k), lambda j, i: (i, 0)),
                  pl.BlockSpec((k, tn), lambda j, i: (0, j)),
                  pl.BlockSpec((rows.tm, LANE), lambda j, i: (i, 0)),
                  pl.BlockSpec((rows.tm, LANE), lambda j, i: (i, 0)),
                  pl.BlockSpec(gains.shape, lambda j, i: (0, 0))],
        out_specs=out_specs,
        out_shape=out_shape,
        compiler_params=_params(("arbitrary", "arbitrary")),
        name="in_projection",
    )(x, w, cos, sin, gains)


def _softmax_terms(s_list, m_list, guard):
    sm = [s if m is None else jnp.where(m, s, NEG) for s, m in zip(s_list, m_list)]
    mx = jnp.max(sm[0], axis=-1, keepdims=True)
    for x in sm[1:]:
        mx = jnp.maximum(mx, jnp.max(x, axis=-1, keepdims=True))
    es = [jnp.exp(x - mx) for x in sm]
    if guard:
        es = [jnp.where(m, e, 0.0) for e, m in zip(es, m_list)]
    tot = jnp.sum(es[0], axis=-1, keepdims=True)
    for e in es[1:]:
        tot = tot + jnp.sum(e, axis=-1, keepdims=True)
    inv = jnp.where(tot > 0.0, 1.0 / tot, 0.0) if guard else 1.0 / tot
    return es, inv


def _attend(pieces):
    s_list = [_dot_t(q, k) for q, k, _, _ in pieces]
    es, inv = _softmax_terms(s_list, [m for _, _, _, m in pieces], guard=False)
    o = _dot(es[0].astype(BF16), pieces[0][2])
    for e, (_, _, v, _) in zip(es[1:], pieces[1:]):
        o = o + _dot(e.astype(BF16), v)
    return o * inv


def _compress(z, wphi_ref, gk_ref, cosc_ref, sinc_ref, ck_ref, cv_ref):
    w0 = wphi_ref[0].astype(BF16)
    w1 = wphi_ref[1].astype(BF16)
    for g in range(NSA_KV_HEADS):
        zk = z[:, g * LANE:(g + 1) * LANE].astype(BF16)
        zv = z[:, (NSA_KV_HEADS + g) * LANE:(NSA_KV_HEADS + g + 1) * LANE].astype(BF16)
        ck = _rope(_rms(_dot(zk, w0), gk_ref[0:1, :]), cosc_ref[...], sinc_ref[...])
        ck_ref[g] = ck.astype(BF16)
        cv_ref[g] = _dot(zv, w1).astype(BF16)


def _pool(rows_f32, pe_t):
    n = rows_f32.shape[0] // CMP_BLOCK
    x3 = rows_f32.reshape(n, CMP_BLOCK, rows_f32.shape[1]) * pe_t[None]
    return jnp.sum(x3, axis=1) * (1.0 / CMP_BLOCK)


def _select_blocks(p_c, t1, q_rows, n_sel_blocks):
    qn = q_rows
    imp = p_c[0:qn] + p_c[qn:2 * qn] + p_c[2 * qn:3 * qn] + p_c[3 * qn:4 * qn]
    pair = imp + pltpu.roll(imp, LANE - 1, 1)
    lane = lax.broadcasted_iota(jnp.int32, (qn, LANE), 1)
    blk = lane >> 1
    lane_ok = ((lane & 1) == 0) & (lane < 2 * n_sel_blocks)
    cur = t1 >> 6
    forced = (blk == 0) | (blk == cur) | (blk == cur - 1)
    valid = lane_ok & ((blk << 6) <= t1)
    score = jnp.where(valid, pair + jnp.where(forced, FORCE, 0.0), -1.0)
    rank = jnp.zeros((qn, LANE), F32)
    for b in range(n_sel_blocks):
        col = score[:, 2 * b:2 * b + 1]
        beats = (col > score) | ((col == score) & (lane > 2 * b))
        rank = rank + jnp.where(beats, 1.0, 0.0)
    sel = (rank < float(min(SEL_TOPK, n_sel_blocks))) & (score >= 0.0)
    return jnp.where(sel, 1.0, 0.0)


def _nsa_group(qg, t1, ck, cv, n_cmp, n_sel_blocks, sel_pieces, win_pieces):
    qn = t1.shape[0]
    t4 = jnp.concatenate([t1] * NSA_GROUP, axis=0)
    s = _dot_t(qg, ck)
    lane = lax.broadcasted_iota(jnp.int32, (1, NCMP_PAD), 1)
    cpos = jnp.where(lane < n_cmp, lane * CMP_BLOCK + (CMP_BLOCK - 1), jnp.int32(2 ** 30))
    es, inv = _softmax_terms([s], [cpos <= t4], guard=True)
    p_c = es[0] * inv
    o_c = _dot(p_c.astype(BF16), cv)
    sel = _select_blocks(p_c, t1, qn, n_sel_blocks)
    sel4 = jnp.concatenate([sel] * NSA_GROUP, axis=0)
    bias = jnp.where(sel4 > 0.5, 0.0, NEG)
    pieces = []
    for kind, k, v, extra in sel_pieces:
        if kind == "bias":
            b = bias if extra is None else jnp.minimum(bias, extra)
            pieces.append((jnp.concatenate([qg, b.astype(BF16)], axis=1), k, v, None))
        elif kind == "mask":
            pieces.append((jnp.concatenate([qg, bias.astype(BF16)], axis=1), k, v, extra <= t4))
        else:
            block_lane, kpos = extra
            flag = sel4[:, block_lane:block_lane + 1]
            pieces.append((qg, k, v, jnp.where(kpos <= t4, flag, 0.0) > 0.5))
    o_s = _attend(pieces)
    pieces = []
    for k, v, kpos in win_pieces:
        dist = t4 - kpos
        pieces.append((qg, k, v, (dist >= 0) & (dist < WINDOW) & (kpos >= 0)))
    o_w = _attend(pieces)
    return o_c, o_s, o_w


def _nsa_write(o_ref, gates, g, qn, o_c, o_s, o_w):
    for h in range(NSA_GROUP):
        hh = g * NSA_GROUP + h
        r = slice(h * qn, (h + 1) * qn)
        o = (gates[:, 3 * hh:3 * hh + 1] * o_c[r] + gates[:, 3 * hh + 1:3 * hh + 2] * o_s[r]
             + gates[:, 3 * hh + 2:3 * hh + 3] * o_w[r])
        o_ref[:, hh * LANE:(hh + 1) * LANE] = o.astype(o_ref.dtype)


def _lambda(lam_ref, lam_init):
    lv = lam_ref[...]
    return (jnp.exp(jnp.sum(lv[0:1] * lv[1:2], axis=-1, keepdims=True))
            - jnp.exp(jnp.sum(lv[2:3] * lv[3:4], axis=-1, keepdims=True)) + lam_init)


def _diff_head(q1, q2, pieces, lam, gain, lam_init):
    masks = [m for _, _, _, m in pieces]
    e1, inv1 = _softmax_terms([_dot_t(q1, k1) for k1, _, _, _ in pieces], masks, guard=False)
    e2, inv2 = _softmax_terms([_dot_t(q2, k2) for _, k2, _, _ in pieces], masks, guard=False)
    o = None
    for a1, a2, (_, _, v, _) in zip(e1, e2, pieces):
        a = a1 * inv1 - lam * (a2 * inv2)
        term = _dot(a.astype(BF16), v)
        o = term if o is None else o + term
    return _rms(o, gain) * (1.0 - lam_init)


Q_NSA = 128
Q_DIFF = 256
WIN_KEYS = WINDOW + Q_NSA
KEY_STEP = 512


def _nsa_prompt_kernel(seq, q_ref, g_ref, cmp_ref, kv_ref, pe_ref, wphi_ref, gk_ref, cosc_ref, sinc_ref,
                       et_ref, o_ref, ck_ref, cv_ref, kaug_ref):
    i = pl.program_id(1)

    @pl.when(i == 0)
    def _():
        z = _pool(cmp_ref[...], pe_ref[...])
        z = jnp.concatenate([z, jnp.zeros((NCMP_PAD - z.shape[0], z.shape[1]), F32)], axis=0)
        _compress(z, wphi_ref, gk_ref, cosc_ref, sinc_ref, ck_ref, cv_ref)
        for g in range(NSA_KV_HEADS):
            kaug_ref[g, :, 0:LANE] = kv_ref[:, g * LANE:(g + 1) * LANE]
            kaug_ref[g, :, LANE:2 * LANE] = et_ref[...]

    t0 = pl.multiple_of(i * Q_NSA, Q_NSA)
    t1 = t0 + lax.broadcasted_iota(jnp.int32, (Q_NSA, 1), 0)
    start = pl.multiple_of(jnp.clip(t0 - WINDOW, 0, seq - WIN_KEYS), Q_NSA)
    wpos = start + lax.broadcasted_iota(jnp.int32, (1, WIN_KEYS), 1)
    dpos = t0 + lax.broadcasted_iota(jnp.int32, (1, Q_NSA), 1)
    lane = lax.broadcasted_iota(jnp.int32, (1, LANE), 1)
    first_blk = t0 // SEL_BLOCK
    diag = (lane >= 2 * first_blk) & (lane < 2 * (first_blk + Q_NSA // SEL_BLOCK))
    drop_diag = jnp.where(diag, NEG, 0.0)
    gates = g_ref[...].astype(F32)

    def body(n_keys):
        for g in range(NSA_KV_HEADS):
            qg = jnp.concatenate([q_ref[:, (g * NSA_GROUP + h) * LANE:(g * NSA_GROUP + h + 1) * LANE]
                                  for h in range(NSA_GROUP)], axis=0)
            sel_pieces = [("bias", kaug_ref[g, 0:n_keys, :], kv_ref[0:n_keys, (2 + g) * LANE:(3 + g) * LANE],
                           drop_diag),
                          ("mask", kaug_ref[g, pl.ds(t0, Q_NSA), :],
                           kv_ref[pl.ds(t0, Q_NSA), (2 + g) * LANE:(3 + g) * LANE], dpos)]
            win_pieces = [(kv_ref[pl.ds(start, WIN_KEYS), (4 + g) * LANE:(5 + g) * LANE],
                           kv_ref[pl.ds(start, WIN_KEYS), (6 + g) * LANE:(7 + g) * LANE], wpos)]
            o_c, o_s, o_w = _nsa_group(qg, t1, ck_ref[g], cv_ref[g], seq // CMP_BLOCK, seq // SEL_BLOCK,
                                       sel_pieces, win_pieces)
            _nsa_write(o_ref, gates, g, Q_NSA, o_c, o_s, o_w)

    need = (t0 + Q_NSA + KEY_STEP - 1) // KEY_STEP
    for nk in range(1, seq // KEY_STEP + 1):
        pl.when(need == nk)(functools.partial(body, nk * KEY_STEP))


def _nsa_prompt(q, gates, nsa_rows, kv_bf, pe_t, w_phi, g_nsa_k, cosc, sinc, e_mat, layer, batch, seq):
    m = q.shape[0]
    nq = seq // Q_NSA
    gate_blk = gates.shape[1] // LANE - 1
    return pl.pallas_call(
        functools.partial(_nsa_prompt_kernel, seq),
        grid=(batch, nq),
        in_specs=[pl.BlockSpec((Q_NSA, NSA_HEADS * HEAD_DIM), lambda b, i: (b * nq + i, 0)),
                  pl.BlockSpec((Q_NSA, LANE), lambda b, i: (b * nq + i, gate_blk)),
                  pl.BlockSpec((seq, 4 * LANE), lambda b, i: (b, 0)),
                  pl.BlockSpec((seq, 8 * LANE), lambda b, i: (b, 0)),
                  pl.BlockSpec((None, CMP_BLOCK, 4 * LANE), lambda b, i: (layer, 0, 0)),
                  pl.BlockSpec((None, 2, HEAD_DIM, HEAD_DIM), lambda b, i: (layer, 0, 0, 0)),
                  pl.BlockSpec((None, 3, HEAD_DIM), lambda b, i: (layer, 0, 0)),
                  pl.BlockSpec((NCMP_PAD, LANE), lambda b, i: (0, 0)),
                  pl.BlockSpec((NCMP_PAD, LANE), lambda b, i: (0, 0)),
                  pl.BlockSpec(e_mat.shape, lambda b, i: (0, 0))],
        out_specs=pl.BlockSpec((Q_NSA, NSA_HEADS * HEAD_DIM), lambda b, i: (b * nq + i, 0)),
        out_shape=jax.ShapeDtypeStruct((m, NSA_HEADS * HEAD_DIM), BF16),
        scratch_shapes=[pltpu.VMEM((NSA_KV_HEADS, NCMP_PAD, HEAD_DIM), BF16),
                        pltpu.VMEM((NSA_KV_HEADS, NCMP_PAD, HEAD_DIM), BF16),
                        pltpu.VMEM((NSA_KV_HEADS, seq, 2 * LANE), BF16)],
        compiler_params=_params(("arbitrary", "arbitrary")),
        name="nsa_prompt",
    )(q, gates, nsa_rows, kv_bf, pe_t, w_phi, g_nsa_k, cosc, sinc, e_mat)


def _diff_prompt_kernel(seq, lam_init, q_ref, kv_ref, lam_ref, gd_ref, o_ref):
    t0 = pl.program_id(1) * Q_DIFF
    t1 = t0 + lax.broadcasted_iota(jnp.int32, (Q_DIFF, 1), 0)
    lam = _lambda(lam_ref, lam_init)
    vbase = DIFF_HEADS * DIFF_V

    def body(n_keys):
        mask = lax.broadcasted_iota(jnp.int32, (1, n_keys), 1) <= t1
        for h in range(DIFF_HEADS):
            c = 2 * h * LANE
            pieces = [(kv_ref[0:n_keys, c:c + LANE], kv_ref[0:n_keys, c + LANE:c + 2 * LANE],
                       kv_ref[0:n_keys, vbase + h * DIFF_V:vbase + (h + 1) * DIFF_V], mask)]
            o = _diff_head(q_ref[:, c:c + LANE], q_ref[:, c + LANE:c + 2 * LANE], pieces, lam, gd_ref[...],
                           lam_init)
            o_ref[:, h * DIFF_V:(h + 1) * DIFF_V] = o.astype(o_ref.dtype)

    need = (t0 + Q_DIFF + KEY_STEP - 1) // KEY_STEP
    for nk in range(1, seq // KEY_STEP + 1):
        pl.when(need == nk)(functools.partial(body, nk * KEY_STEP))


def _diff_prompt(q, kv_bf, lam_diff, g_dout3, layer, lam_init, batch, seq):
    m = q.shape[0]
    nq = seq // Q_DIFF
    width = DIFF_HEADS * DIFF_V
    return pl.pallas_call(
        functools.partial(_diff_prompt_kernel, seq, lam_init),
        grid=(batch, nq),
        in_specs=[pl.BlockSpec((Q_DIFF, width), lambda b, i: (b * nq + i, 0)),
                  pl.BlockSpec((seq, 2 * width), lambda b, i: (b, 0)),
                  pl.BlockSpec((None, 4, DIFF_QK), lambda b, i: (layer, 0, 0)),
                  pl.BlockSpec((None, 1, DIFF_V), lambda b, i: (layer, 0, 0))],
        out_specs=pl.BlockSpec((Q_DIFF, width), lambda b, i: (b * nq + i, 0)),
        out_shape=jax.ShapeDtypeStruct((m, width), BF16),
        compiler_params=_params(("arbitrary", "arbitrary")),
        name="diff_prompt",
    )(q, kv_bf, lam_diff, g_dout3)


def _pad_rows(x, n):
    return jnp.concatenate([x, jnp.zeros((n - x.shape[0], x.shape[1]), x.dtype)], axis=0)


def _nsa_sample_kernel(n_pages, page, steps, pt_ref, *refs):
    pages = refs[:n_pages]
    (q_ref, g_ref, new_ref, wnew_ref, wbuf_ref, pe_ref, wphi_ref, gk_ref, cosc_ref, sinc_ref, et_ref,
     o_ref, kaug, vbuf, zbuf, ck_ref, cv_ref) = refs[n_pages:]
    past = n_pages * page
    per_page = page // CMP_BLOCK
    pe_t = pe_ref[...]

    @pl.when(pl.program_id(0) == 0)
    def _():
        for g in range(NSA_KV_HEADS):
            kaug[g, :, LANE:2 * LANE] = et_ref[...]

    zbuf[...] = jnp.zeros(zbuf.shape, F32)
    for p in range(n_pages):
        rows_p = slice(p * page, (p + 1) * page)
        for c in range(4):
            cs = slice(c * LANE, (c + 1) * LANE)
            zbuf[p * per_page:(p + 1) * per_page, cs] = _pool(pages[p][pl.ds(c, page, stride=8), :], pe_t[:, cs])
        for g in range(NSA_KV_HEADS):
            kaug[g, rows_p, 0:LANE] = pages[p][pl.ds(4 + g, page, stride=8), :].astype(BF16)
            vbuf[rows_p, g * LANE:(g + 1) * LANE] = pages[p][pl.ds(6 + g, page, stride=8), :].astype(BF16)
    new = new_ref[...]
    z_new = jnp.sum(new[:, 0:4 * LANE] * pe_t[0:steps, :], axis=0, keepdims=True) * (1.0 / CMP_BLOCK)
    zbuf[past // CMP_BLOCK:past // CMP_BLOCK + 1, :] = z_new
    _compress(zbuf[...], wphi_ref, gk_ref, cosc_ref, sinc_ref, ck_ref, cv_ref)

    padded = -(-(past + steps) // SEL_BLOCK) * SEL_BLOCK
    n_cmp = padded // CMP_BLOCK
    n_sel = padded // SEL_BLOCK
    t1 = past + lax.broadcasted_iota(jnp.int32, (steps, 1), 0)
    kpos_new = past + lax.broadcasted_iota(jnp.int32, (1, LANE), 1)
    kpos_new = jnp.where(kpos_new < past + steps, kpos_new, jnp.int32(2 ** 30))
    w_buf = wbuf_ref.shape[0] // 4
    wpos_past = (past - w_buf) + lax.broadcasted_iota(jnp.int32, (1, w_buf), 1)
    gates = g_ref[...]
    wnew = wnew_ref[...]
    q = q_ref[...]
    for g in range(NSA_KV_HEADS):
        qg = jnp.concatenate([q[:, (g * NSA_GROUP + h) * LANE:(g * NSA_GROUP + h + 1) * LANE]
                              for h in range(NSA_GROUP)], axis=0).astype(BF16)
        k_new = _pad_rows(new[:, (4 + g) * LANE:(5 + g) * LANE], LANE).astype(BF16)
        v_new = _pad_rows(new[:, (6 + g) * LANE:(7 + g) * LANE], LANE).astype(BF16)
        sel_pieces = [("bias", kaug[g], vbuf[:, g * LANE:(g + 1) * LANE], None),
                      ("lane", k_new, v_new, (2 * (past // SEL_BLOCK), kpos_new))]
        wk_new = _pad_rows(wnew[:, g * LANE:(g + 1) * LANE], LANE).astype(BF16)
        wv_new = _pad_rows(wnew[:, (2 + g) * LANE:(3 + g) * LANE], LANE).astype(BF16)
        win_pieces = [(wbuf_ref[pl.ds(g, w_buf, stride=4), :].astype(BF16),
                       wbuf_ref[pl.ds(2 + g, w_buf, stride=4), :].astype(BF16), wpos_past),
                      (wk_new, wv_new, kpos_new)]
        o_c, o_s, o_w = _nsa_group(qg, t1, ck_ref[g], cv_ref[g], n_cmp, n_sel, sel_pieces, win_pieces)
        _nsa_write(o_ref, gates, g, steps, o_c, o_s, o_w)


def _nsa_sample(page_table, cache4, q3, gates3, new3, wnew3, win_state, pe_t, w_phi, g_nsa_k, cosc, sinc,
                e_mat, layer):
    n_seq, n_pages = page_table.shape
    page = cache4.shape[2] // 8
    steps = q3.shape[1]
    past = n_pages * page
    gate_blk = gates3.shape[2] // LANE - 1
    w_rows = win_state.shape[2]

    def page_spec(p):
        return pl.BlockSpec((None, None, 8 * page, LANE), lambda s, pt: (layer, pt[s, p], 0, 0))

    def seq_spec(width, blk=0):
        return pl.BlockSpec((None, steps, width), lambda s, pt: (s, 0, blk))

    def const_spec(shape):
        return pl.BlockSpec(shape, lambda s, pt: (0,) * len(shape))

    grid_spec = pltpu.PrefetchScalarGridSpec(
        num_scalar_prefetch=1,
        grid=(n_seq,),
        in_specs=[page_spec(p) for p in range(n_pages)] + [
            seq_spec(NSA_HEADS * HEAD_DIM), seq_spec(LANE, gate_blk), seq_spec(8 * LANE), seq_spec(4 * LANE),
            pl.BlockSpec((None, None, w_rows, LANE), lambda s, pt: (layer, s, 0, 0)),
            pl.BlockSpec((None, CMP_BLOCK, 4 * LANE), lambda s, pt: (layer, 0, 0)),
            pl.BlockSpec((None, 2, HEAD_DIM, HEAD_DIM), lambda s, pt: (layer, 0, 0, 0)),
            pl.BlockSpec((None, 3, HEAD_DIM), lambda s, pt: (layer, 0, 0)),
            const_spec((NCMP_PAD, LANE)), const_spec((NCMP_PAD, LANE)), const_spec(e_mat.shape)],
        out_specs=pl.BlockSpec((None, steps, NSA_HEADS * HEAD_DIM), lambda s, pt: (s, 0, 0)),
        scratch_shapes=[pltpu.VMEM((NSA_KV_HEADS, past, 2 * LANE), BF16),
                        pltpu.VMEM((past, NSA_KV_HEADS * LANE), BF16),
                        pltpu.VMEM((NCMP_PAD, 4 * LANE), F32),
                        pltpu.VMEM((NSA_KV_HEADS, NCMP_PAD, HEAD_DIM), BF16),
                        pltpu.VMEM((NSA_KV_HEADS, NCMP_PAD, HEAD_DIM), BF16)])
    return pl.pallas_call(
        functools.partial(_nsa_sample_kernel, n_pages, page, steps),
        grid_spec=grid_spec,
        out_shape=jax.ShapeDtypeStruct((n_seq, steps, NSA_HEADS * HEAD_DIM), F32),
        compiler_params=_params(("arbitrary",)),
        name="nsa_sample",
    )(page_table, *([cache4] * n_pages), q3, gates3, new3, wnew3, win_state, pe_t, w_phi, g_nsa_k, cosc, sinc,
      e_mat)


def _token_rows(ref3, j):
    tokens = ref3.shape[0]
    return ref3.reshape(tokens * 8, LANE)[pl.ds(j, tokens, stride=8), :]


def _diff_sample_kernel(n_pages, page, steps, lam_init, pt_ref, *refs):
    pages = refs[:2 * n_pages]
    q_ref, new_ref, lam_ref, gd_ref, o_ref, kvbuf = refs[2 * n_pages:]
    past = n_pages * page
    for p in range(n_pages):
        for c in range(2 * DIFF_HEADS * 2):
            slot, head, half = c // 8, (c // 2) % DIFF_HEADS, c % 2
            kvbuf[p * page:(p + 1) * page, c * LANE:(c + 1) * LANE] = _token_rows(
                pages[slot * n_pages + p], 4 * half + head).astype(BF16)
    t1 = past + lax.broadcasted_iota(jnp.int32, (steps, 1), 0)
    mask_past = lax.broadcasted_iota(jnp.int32, (1, past), 1) <= t1
    mask_new = (past + lax.broadcasted_iota(jnp.int32, (1, LANE), 1)) <= t1
    lam = _lambda(lam_ref, lam_init)
    new = new_ref[...]
    q = q_ref[...].astype(BF16)
    vbase = DIFF_HEADS * DIFF_V
    for h in range(DIFF_HEADS):
        c = 2 * h * LANE
        k1n = _pad_rows(new[:, c:c + LANE], LANE).astype(BF16)
        k2n = _pad_rows(new[:, c + LANE:c + 2 * LANE], LANE).astype(BF16)
        vn = _pad_rows(new[:, vbase + h * DIFF_V:vbase + (h + 1) * DIFF_V], LANE).astype(BF16)
        pieces = [(kvbuf[:, c:c + LANE], kvbuf[:, c + LANE:c + 2 * LANE],
                   kvbuf[:, vbase + h * DIFF_V:vbase + (h + 1) * DIFF_V], mask_past),
                  (k1n, k2n, vn, mask_new)]
        o = _diff_head(q[:, c:c + LANE], q[:, c + LANE:c + 2 * LANE], pieces, lam, gd_ref[...], lam_init)
        o_ref[:, h * DIFF_V:(h + 1) * DIFF_V] = o.astype(o_ref.dtype)


def _diff_sample(page_table, cache4, q3, new3, lam_diff, g_dout3, layer, lam_init):
    n_seq, n_pages = page_table.shape
    page = cache4.shape[2]
    steps = q3.shape[1]
    width = DIFF_HEADS * DIFF_V
    grid_spec = pltpu.PrefetchScalarGridSpec(
        num_scalar_prefetch=1,
        grid=(n_seq,),
        in_specs=[pl.BlockSpec((None, None, page, 8, LANE), functools.partial(
            lambda slot, p, s, pt: (layer, pt[s, p], 0, slot, 0), slot, p))
            for slot in range(2) for p in range(n_pages)] + [
            pl.BlockSpec((None, steps, width), lambda s, pt: (s, 0, 0)),
            pl.BlockSpec((None, steps, 2 * width), lambda s, pt: (s, 0, 0)),
            pl.BlockSpec((None, 4, DIFF_QK), lambda s, pt: (layer, 0, 0)),
            pl.BlockSpec((None, 1, DIFF_V), lambda s, pt: (layer, 0, 0))],
        out_specs=pl.BlockSpec((None, steps, width), lambda s, pt: (s, 0, 0)),
        scratch_shapes=[pltpu.VMEM((n_pages * page, 2 * width), BF16)])
    return pl.pallas_call(
        functools.partial(_diff_sample_kernel, n_pages, page, steps, lam_init),
        grid_spec=grid_spec,
        out_shape=jax.ShapeDtypeStruct((n_seq, steps, width), F32),
        compiler_params=_params(("arbitrary",)),
        name="diff_sample",
    )(page_table, *([cache4] * (2 * n_pages)), q3, new3, lam_diff, g_dout3)


def _merge_kernel(a_ref, b_ref, wa_ref, wb_ref, ga_ref, gb_ref, o_ref, wa_s, wb_s):
    @pl.when(pl.program_id(1) == 0)
    def _():
        wa_s[...] = wa_ref[...].astype(BF16)
        wb_s[...] = wb_ref[...].astype(BF16)

    oa = _dot(a_ref[...].astype(BF16), wa_s[...])
    ob = _dot(b_ref[...].astype(BF16), wb_s[...])
    o_ref[...] = (ga_ref[...] * oa + gb_ref[...] * ob).astype(o_ref.dtype)


def _merge(attn_a, attn_b, w_branch, gates, layer, rows):
    m, ka = attn_a.shape
    kb = attn_b.shape[1]
    d = w_branch.shape[3]
    tn = 1024
    nj = d // tn
    return pl.pallas_call(
        _merge_kernel,
        grid=(nj, m // rows.tm),
        in_specs=[pl.BlockSpec((rows.tm, ka), lambda j, i: (i, 0)),
                  pl.BlockSpec((rows.tm, kb), lambda j, i: (i, 0)),
                  pl.BlockSpec((None, None, ka, tn), lambda j, i: (layer, 0, 0, j)),
                  pl.BlockSpec((None, None, kb, tn), lambda j, i: (layer, 1, 0, j)),
                  pl.BlockSpec((rows.tm, tn), lambda j, i: (i, j)),
                  pl.BlockSpec((rows.tm, tn), lambda j, i: (i, nj + j))],
        out_specs=pl.BlockSpec((rows.tm, tn), lambda j, i: (i, j)),
        out_shape=jax.ShapeDtypeStruct((m, d), BF16),
        scratch_shapes=[pltpu.VMEM((ka, tn), BF16), pltpu.VMEM((kb, tn), BF16)],
        compiler_params=_params(("arbitrary", "arbitrary")),
        name="merge_branches",
    )(attn_a, attn_b, w_branch, w_branch, gates, gates)


def _resid_kernel(rows, cast_w, a_ref, w_ref, x_ref, gt_ref, o_ref, *scratch):
    if cast_w:
        w_s, = scratch

        @pl.when(pl.program_id(1) == 0)
        def _():
            w_s[...] = w_ref[...].astype(BF16)

        w = w_s[...]
    else:
        w = w_ref[...]
    gt = _rows_bcast(gt_ref[...], rows, o_ref.shape[1])
    o_ref[...] = x_ref[...] + gt * _dot(a_ref[...], w)


def _resid_proj(a, w, w_layer, x2, mod3, comp_blocks, rows, tn):
    m, k = a.shape
    n = x2.shape[1]
    cast_w = w_layer is not None
    if cast_w:
        w_spec = pl.BlockSpec((None, k, tn), lambda j, i: (w_layer, 0, j))
        scratch = [pltpu.VMEM((k, tn), BF16)]
    else:
        w_spec = pl.BlockSpec((k, tn), lambda j, i: (0, j))
        scratch = []
    off = comp_blocks * (n // tn)
    return pl.pallas_call(
        functools.partial(_resid_kernel, rows, cast_w),
        grid=(n // tn, m // rows.tm),
        in_specs=[pl.BlockSpec((rows.tm, k), lambda j, i: (i, 0)),
                  w_spec,
                  pl.BlockSpec((rows.tm, tn), lambda j, i: (i, j)),
                  pl.BlockSpec((rows.bb, 1, tn), lambda j, i: (i // rows.mod_div, 0, off + j))],
        out_specs=pl.BlockSpec((rows.tm, tn), lambda j, i: (i, j)),
        out_shape=jax.ShapeDtypeStruct((m, n), F32),
        scratch_shapes=scratch,
        compiler_params=_params(("arbitrary", "arbitrary")),
        name="residual_projection",
    )(a, w, x2, mod3)


def _ffn_up_kernel(rows, carry_prev, h_ref, wg_ref, wv_ref, cw_ref, cb_ref, prev_ref, act_ref, st_ref, *scratch):
    tn = act_ref.shape[1]
    h = h_ref[...]
    ug = _dot(h, wg_ref[...])
    uv = _dot(h, wv_ref[...])
    ug3 = ug.reshape(rows.bb, rows.tt, tn)
    if carry_prev:
        carry, = scratch

        @pl.when(pl.program_id(1) % rows.mod_div == 0)
        def _():
            carry[...] = prev_ref[...]

        prev = carry[...]
    else:
        prev = prev_ref[...]
    p0 = prev[:, 0:1, :]
    p1 = prev[:, 1:2, :]
    rid = lax.broadcasted_iota(jnp.int32, ug3.shape, 1)
    u1 = jnp.where(rid == 0, p1, pltpu.roll(ug3, 1, 1))
    u2 = jnp.where(rid == 0, p0, jnp.where(rid == 1, p1, pltpu.roll(ug3, 2, 1)))
    cw = cw_ref[...]
    conv = cb_ref[...] + u2 * cw[0:1, :] + u1 * cw[1:2, :] + ug3 * cw[2:3, :]
    act = jax.nn.gelu(conv).reshape(rows.tm, tn) * uv
    act_ref[...] = act.astype(act_ref.dtype)
    tail = ug3[:, rows.tt - (CONV_W - 1):rows.tt, :]
    st_ref[...] = tail
    if carry_prev:
        carry[...] = tail


def _ffn_up(h, w_up_p, conv_w_p, conv_b_p, prev, rows, carry_prev):
    m, k = h.shape
    fp = w_up_p.shape[1] // 2
    tn = TN_FF
    nj = fp // tn
    n_seq = prev.shape[0]
    scratch = [pltpu.VMEM((1, CONV_W - 1, tn), F32)] if carry_prev else []
    return pl.pallas_call(
        functools.partial(_ffn_up_kernel, rows, carry_prev),
        grid=(nj, m // rows.tm),
        in_specs=[pl.BlockSpec((rows.tm, k), lambda j, i: (i, 0)),
                  pl.BlockSpec((k, tn), lambda j, i: (0, j)),
                  pl.BlockSpec((k, tn), lambda j, i: (0, nj + j)),
                  pl.BlockSpec((CONV_W, tn), lambda j, i: (0, j)),
                  pl.BlockSpec((1, tn), lambda j, i: (0, j)),
                  pl.BlockSpec((rows.bb, CONV_W - 1, tn), lambda j, i: (i // rows.mod_div, 0, j))],
        out_specs=[pl.BlockSpec((rows.tm, tn), lambda j, i: (i, j)),
                   pl.BlockSpec((rows.bb, CONV_W - 1, tn), lambda j, i: (i // rows.mod_div, 0, j))],
        out_shape=[jax.ShapeDtypeStruct((m, fp), BF16),
                   jax.ShapeDtypeStruct((n_seq, CONV_W - 1, fp), F32)],
        scratch_shapes=scratch,
        compiler_params=_params(("arbitrary", "arbitrary")),
        name="ffn_up_conv",
    )(h, w_up_p, w_up_p, conv_w_p, conv_b_p, prev)


def _rope_tables(pos):
    half = HEAD_DIM // 2
    freqs = ROPE_THETA ** (-jnp.arange(half, dtype=F32) / half)
    ang = pos.astype(F32)[:, None] * freqs
    cos, sin = jnp.cos(ang), jnp.sin(ang)
    return jnp.concatenate([cos, cos], axis=-1), jnp.concatenate([-sin, sin], axis=-1)


def _proj_plans(sample):
    q_scale = HEAD_DIM ** -0.5
    d_scale = DIFF_QK ** -0.5
    plan_a = [("nr", 0, q_scale, ((0, c),)) for c in range(8)]
    if sample:
        plan_a += [("id", 0, 1.0, ((1, c), (3, c))) for c in range(4)]
        plan_a += [("nr", 3, 1.0, ((1, 4 + c), (3, 4 + c))) for c in range(2)]
        plan_a += [("id", 0, 1.0, ((1, 6 + c), (3, 6 + c))) for c in range(2)]
        plan_a += [("nr", 4, 1.0, ((2, c), (4, c))) for c in range(2)]
        plan_a += [("id", 0, 1.0, ((2, 2 + c), (4, 2 + c))) for c in range(2)]
    else:
        plan_a += [("id", 0, 1.0, ((1, c), (4, c))) for c in range(4)]
        plan_a += [("nr", 3, 1.0, ((1, 4 + c), (3, c))) for c in range(2)]
        plan_a += [("id", 0, 1.0, ((1, 6 + c), (3, 2 + c))) for c in range(2)]
        plan_a += [("nr", 4, 1.0, ((2, c), (3, 4 + c))) for c in range(2)]
        plan_a += [("id", 0, 1.0, ((2, 2 + c), (3, 6 + c))) for c in range(2)]
    plan_b = [("nr", 5, d_scale, ((0, c),)) for c in range(8)]
    plan_b += [("nr", 6, 1.0, ((1, 4 * (c % 2) + c // 2), (2, c))) for c in range(8)]
    plan_b += [("id", 0, 1.0, ((1, 8 + 4 * (c % 2) + c // 2), (2, 8 + c))) for c in range(8)]
    return plan_a, plan_b


def _win_shift_kernel(st_ref, new_ref, o_ref):
    total = st_ref.shape[1]
    fresh = new_ref.shape[1]
    o_ref[:, 0:total - fresh, :] = st_ref[:, fresh:total, :]
    o_ref[:, total - fresh:total, :] = new_ref[...]


def _win_shift(win_state4, new_rows):
    depth, n_seq, total, _ = win_state4.shape
    fresh = new_rows.shape[2]
    bs = 4
    return pl.pallas_call(
        _win_shift_kernel,
        grid=(depth, n_seq // bs),
        in_specs=[pl.BlockSpec((None, bs, total, LANE), lambda l, s: (l, s, 0, 0)),
                  pl.BlockSpec((None, bs, fresh, LANE), lambda l, s: (l, s, 0, 0))],
        out_specs=pl.BlockSpec((None, bs, total, LANE), lambda l, s: (l, s, 0, 0)),
        out_shape=jax.ShapeDtypeStruct(win_state4.shape, F32),
        compiler_params=_params(("arbitrary", "arbitrary")),
        name="window_shift",
    )(win_state4, new_rows)


def kernel(x_prompt, x_sample, cache_nsa_kv, cache_diff_kv, state_win_kv, state_conv, page_table, c_prompt,
           c_sample, w_ada, b_ada, norm1_g, norm2_g, w_in, g_nsa_q, g_nsa_k, pe_cmp, w_phi, g_diff_q, g_diff_k,
           lam_diff, g_diff_out, w_branch, w_out, w_up, conv_w, conv_b, w_down):
    batch, seq, d = x_prompt.shape
    n_seq, steps, _ = x_sample.shape
    depth = w_in.shape[0]
    n_pages, page = page_table.shape[1], cache_nsa_kv.shape[2]
    past = n_pages * page
    n_pool = cache_nsa_kv.shape[1]
    d_ff = w_down.shape[1]
    fp = -(-d_ff // TN_FF) * TN_FF

    rows_p = Rows(batch * seq, TM, 1, TM, seq // TM)
    rows_pf = Rows(batch * seq, TM_FF, 1, TM_FF, seq // TM_FF)
    tm_s = min(TM, n_seq * steps)
    rows_s = Rows(n_seq * steps, tm_s, tm_s // steps, steps, 1)

    pos_p = jnp.tile(jnp.arange(seq), batch)
    pos_s = jnp.tile(past + jnp.arange(steps), n_seq)
    cos_p, sin_p = _rope_tables(pos_p)
    cos_s, sin_s = _rope_tables(pos_s)
    cosc, sinc = _rope_tables(jnp.arange(NCMP_PAD) * CMP_BLOCK + (CMP_BLOCK - 1))
    e_p = (jnp.arange(LANE)[None, :] == 2 * (jnp.arange(seq)[:, None] // SEL_BLOCK)).astype(BF16)
    e_s = (jnp.arange(LANE)[None, :] == 2 * (jnp.arange(past)[:, None] // SEL_BLOCK)).astype(BF16)
    assert past % SEL_BLOCK == 0 and steps <= SEL_BLOCK and 2 * (past // SEL_BLOCK) < LANE

    mb = -(-(batch + n_seq) // 8) * 8
    c_all = jnp.pad(jnp.concatenate([c_prompt, c_sample], axis=0), ((0, mb - batch - n_seq), (0, 0)))
    b_ada3 = b_ada[:, None, :]
    norm1_3 = norm1_g[:, None, :]
    norm2_3 = norm2_g[:, None, :]
    g_dout3 = g_diff_out[:, None, :]
    pe_t = jnp.concatenate([pe_cmp[:, 0], pe_cmp[:, 0], pe_cmp[:, 1], pe_cmp[:, 1]], axis=-1)
    cache_nsa4 = cache_nsa_kv.reshape(depth, n_pool, page * 8, LANE)
    cache_diff4 = cache_diff_kv.reshape(depth, n_pool, page, 2, DIFF_HEADS, 2, LANE).transpose(
        0, 1, 2, 3, 5, 4, 6).reshape(depth, n_pool, page, 16, LANE)
    win_state4 = state_win_kv.reshape(depth, n_seq, state_win_kv.shape[2] * 4, LANE)
    plan_g = [("sig", 0, 1.0, ((0, c),)) for c in range(11)]

    n_q = NSA_HEADS * HEAD_DIM
    n_kv = 6 * NSA_KV_HEADS * HEAD_DIM
    n_g = 3 * NSA_HEADS
    n_qd = DIFF_HEADS * 2 * DIFF_QK
    o_g = n_q + n_kv
    o_d = o_g + n_g
    o_m = o_d + 3 * n_qd

    xp = x_prompt.reshape(batch * seq, d)
    xs = x_sample.reshape(n_seq * steps, d)
    outs = [[] for _ in range(8)]
    for l in range(depth):
        lam_init = 0.8 - 0.6 * math.exp(-0.3 * l)
        wl = w_in[l]
        w_a = wl[:, :o_g].astype(BF16)
        w_b = wl[:, o_d:o_m].astype(BF16)
        w_g = jnp.concatenate([wl[:, o_m:], wl[:, o_g:o_d],
                               jnp.zeros((d, LANE - n_g), F32)], axis=1).astype(BF16)
        gains = jnp.stack([g_nsa_q[l], g_nsa_q[l], g_nsa_q[l], g_nsa_k[l, 1], g_nsa_k[l, 2],
                           g_diff_q[l], g_diff_k[l], g_diff_k[l]])
        pad_f = ((0, 0), (0, fp - d_ff))
        w_up_p = jnp.concatenate([jnp.pad(w_up[l][:, :d_ff], pad_f), jnp.pad(w_up[l][:, d_ff:], pad_f)],
                                 axis=1).astype(BF16)
        w_down_p = jnp.pad(w_down[l], ((0, fp - d_ff), (0, 0))).astype(BF16)
        conv_w_p = jnp.pad(conv_w[l], pad_f)
        conv_b_p = jnp.pad(conv_b[l][None, :], pad_f)

        mod = _ada(c_all, w_ada, b_ada3, l)
        mod_p = mod[:batch, None, :]
        mod_s = mod[batch:batch + n_seq, None, :]

        new_x = []
        for grp, (x2, rows, mod3, cos, sin) in enumerate(((xp, rows_p, mod_p, cos_p, sin_p),
                                                          (xs, rows_s, mod_s, cos_s, sin_s))):
            sample = grp == 1
            plan_a, plan_b = _proj_plans(sample)
            h = _norm_mod(x2, norm1_3, mod3, l, 0, rows)
            if sample:
                outs_a = [("cols", 8, F32), ("rows", 8, F32), ("rows", 4, F32), ("cols", 8, F32), ("cols", 4, F32)]
                outs_b = [("cols", 8, F32), ("rows", 16, F32), ("cols", 16, F32)]
            else:
                outs_a = [("cols", 8, BF16), ("rows", 8, F32), ("rows", 4, F32), ("cols", 8, BF16), ("cols", 4, F32)]
                outs_b = [("cols", 8, BF16), ("rows", 16, F32), ("cols", 16, BF16)]
            q_nsa, nsa_tm, win_tm, kv_a, kv_b = _proj(h, w_a, cos, sin, gains, plan_a, outs_a, rows)
            q_diff, diff_tm, dkv = _proj(h, w_b, cos, sin, gains, plan_b, outs_b, rows)
            gates, = _proj(h, w_g, cos, sin, gains, plan_g, [("cols", 11, F32 if sample else BF16)], rows,
                           n_tiles=3)
            if not sample:
                a_nsa = _nsa_prompt(q_nsa, gates, kv_b, kv_a, pe_t, w_phi, g_nsa_k, cosc, sinc, e_p,
                                    l, batch, seq)
                a_diff = _diff_prompt(q_diff, dkv, lam_diff, g_dout3, l, lam_init, batch, seq)
            else:
                a_nsa = _nsa_sample(page_table, cache_nsa4, q_nsa.reshape(n_seq, steps, -1),
                                    gates.reshape(n_seq, steps, -1), kv_a.reshape(n_seq, steps, -1),
                                    kv_b.reshape(n_seq, steps, -1), win_state4, pe_t, w_phi, g_nsa_k,
                                    cosc, sinc, e_s, l).reshape(n_seq * steps, -1)
                a_diff = _diff_sample(page_table, cache_diff4, q_diff.reshape(n_seq, steps, -1),
                                      dkv.reshape(n_seq, steps, -1), lam_diff, g_dout3, l,
                                      lam_init).reshape(n_seq * steps, -1)
            merged = _merge(a_nsa, a_diff, w_branch, gates, l, rows)
            x_mid = _resid_proj(merged, w_out, l, x2, mod3, 2, rows, 1024)
            h2 = _norm_mod(x_mid, norm2_3, mod3, l, 3, rows)
            if not sample:
                prev = jnp.zeros((batch, CONV_W - 1, fp), F32)
            else:
                prev = jnp.pad(state_conv[l], ((0, 0), (0, 0), (0, fp - d_ff)))
            rows_f = rows if sample else rows_pf
            act, conv_state = _ffn_up(h2, w_up_p, conv_w_p, conv_b_p, prev, rows_f, carry_prev=not sample)
            x_new = _resid_proj(act, w_down_p, None, x_mid, mod3, 5, rows_f, 512)
            new_x.append(x_new)

            nb, nt = (n_seq, steps) if sample else (batch, seq)
            r_nsa = nsa_tm.reshape(nb, nt, 4, NSA_KV_HEADS, HEAD_DIM)
            r_diff = diff_tm.reshape(nb, nt, 2, 2, DIFF_HEADS, LANE).transpose(0, 1, 2, 4, 3, 5).reshape(
                nb, nt, 2, DIFF_HEADS, DIFF_V)
            if sample:
                r_win = win_tm.reshape(n_seq, steps * 4, LANE)
            else:
                r_win = win_tm.reshape(nb, nt, 2, NSA_KV_HEADS, HEAD_DIM)[:, nt - min(WINDOW, nt):]
            for slot, val in zip(range(4), (r_nsa, r_diff, r_win, conv_state[:, :, :d_ff])):
                outs[4 * grp + slot].append(val)
        xp, xs = new_x
    outs = [jnp.stack(o) for o in outs]
    outs[6] = _win_shift(win_state4, outs[6]).reshape(state_win_kv.shape)
    return (xp.reshape(batch, seq, d), xs.reshape(n_seq, steps, d)) + tuple(outs)
```

```python
import collections
import functools
import math

import jax
import jax.numpy as jnp
from jax import lax
from jax.experimental import pallas as pl
from jax.experimental.pallas import tpu as pltpu

F32 = jnp.float32
BF16 = jnp.bfloat16

HEAD_DIM = 128
NSA_HEADS = 8
NSA_KV_HEADS = 2
NSA_GROUP = NSA_HEADS // NSA_KV_HEADS
CMP_BLOCK = 32
SEL_BLOCK = 64
SEL_TOPK = 16
WINDOW = 512
DIFF_HEADS = 4
DIFF_QK = 128
DIFF_V = 2 * DIFF_QK
CONV_W = 3
ROPE_THETA = 10000.0
EPS = 1e-6
NEG = -1e30
FORCE = 100.0

LANE = 128
VMEM_LIMIT = 56 * 1024 * 1024
TM = 512
TM_FF = 1024
TN_FF = 512
FF_SUB = 256
NCMP_PAD = 128

Rows = collections.namedtuple("Rows", "m tm bb tt mod_div")


def _params(sem):
    return pltpu.CompilerParams(dimension_semantics=sem, vmem_limit_bytes=VMEM_LIMIT)


def _rows_bcast(v, rows, width):
    if rows.bb == 1:
        return v.reshape(1, width)
    return jnp.broadcast_to(v, (rows.bb, rows.tt, width)).reshape(rows.tm, width)


def _rms(y, gain):
    return y * lax.rsqrt(jnp.mean(y * y, axis=-1, keepdims=True) + EPS) * gain


def _rope(y, cos, sin_signed):
    return y * cos + pltpu.roll(y, HEAD_DIM // 2, 1) * sin_signed


def _dot(a, b):
    return jnp.dot(a, b, preferred_element_type=F32)


def _dot_t(a, b):
    return lax.dot_general(a, b, (((1,), (1,)), ((), ())), preferred_element_type=F32)


def _ada_kernel(c_ref, w_ref, b_ref, o_ref):
    c = c_ref[...]
    a = (c * jax.nn.sigmoid(c)).astype(BF16)
    o_ref[...] = _dot(a, w_ref[...].astype(BF16)) + b_ref[...]


def _ada(c_all, w_ada, b_ada3, layer):
    mb, d = c_all.shape
    n = w_ada.shape[2]
    tn = 1024
    return pl.pallas_call(
        _ada_kernel,
        grid=(n // tn,),
        in_specs=[pl.BlockSpec((mb, d), lambda j: (0, 0)),
                  pl.BlockSpec((None, d, tn), lambda j: (layer, 0, j)),
                  pl.BlockSpec((None, 1, tn), lambda j: (layer, 0, j))],
        out_specs=pl.BlockSpec((mb, tn), lambda j: (0, j)),
        out_shape=jax.ShapeDtypeStruct((mb, n), F32),
        compiler_params=_params(("arbitrary",)),
        name="ada_modulation",
    )(c_all, w_ada, b_ada3)


def _norm_mod_kernel(rows, x_ref, g_ref, sh_ref, sc_ref, o_ref):
    d = x_ref.shape[1]
    y = _rms(x_ref[...], g_ref[...])
    sc = _rows_bcast(sc_ref[...], rows, d)
    sh = _rows_bcast(sh_ref[...], rows, d)
    o_ref[...] = (y * (1.0 + sc) + sh).astype(o_ref.dtype)


def _norm_mod(x2, gain3, mod3, layer, comp, rows):
    m, d = x2.shape
    return pl.pallas_call(
        functools.partial(_norm_mod_kernel, rows),
        grid=(m // rows.tm,),
        in_specs=[pl.BlockSpec((rows.tm, d), lambda i: (i, 0)),
                  pl.BlockSpec((None, 1, d), lambda i: (layer, 0, 0)),
                  pl.BlockSpec((rows.bb, 1, d), lambda i: (i // rows.mod_div, 0, comp)),
                  pl.BlockSpec((rows.bb, 1, d), lambda i: (i // rows.mod_div, 0, comp + 1))],
        out_specs=pl.BlockSpec((rows.tm, d), lambda i: (i, 0)),
        out_shape=jax.ShapeDtypeStruct((m, d), BF16),
        compiler_params=_params(("arbitrary",)),
        name="norm_mod",
    )(x2, gain3, mod3, mod3)


def _proj_kernel(plan, per_token, x_ref, w_ref, cos_ref, sin_ref, gain_ref, *out_refs):
    x = x_ref[...]
    cos = cos_ref[...]
    sin = sin_ref[...]
    sub = 4
    for c0 in range(0, len(plan), sub):
        c1 = min(c0 + sub, len(plan))
        acc = _dot(x, w_ref[:, c0 * LANE:c1 * LANE])
        for c in range(c0, c1):
            kind, gi, post, dests = plan[c]
            y = acc[:, (c - c0) * LANE:(c - c0 + 1) * LANE]
            if kind == "nr":
                y = _rope(_rms(y, gain_ref[gi:gi + 1, :]), cos, sin)
                if post != 1.0:
                    y = y * post
            elif kind == "sig":
                y = jax.nn.sigmoid(y)
            for oi, oc in dests:
                ref = out_refs[oi]
                if per_token[oi]:
                    ref[pl.ds(oc, x.shape[0], stride=per_token[oi]), :] = y.astype(ref.dtype)
                else:
                    ref[:, oc * LANE:(oc + 1) * LANE] = y.astype(ref.dtype)


def _proj(x, w, cos, sin, gains, plan, outs, rows, n_tiles=1):
    m, k = x.shape
    tn = len(plan) * LANE
    out_specs, out_shape = [], []
    for layout, nc, dt in outs:
        if layout == "rows":
            assert n_tiles == 1
            out_specs.append(pl.BlockSpec((rows.tm * nc, LANE), lambda j, i: (i, 0)))
            out_shape.append(jax.ShapeDtypeStruct((m * nc, LANE), dt))
        else:
            out_specs.append(pl.BlockSpec((rows.tm, nc * LANE), lambda j, i: (i, j)))
            out_shape.append(jax.ShapeDtypeStruct((m, n_tiles * nc * LANE), dt))
    return pl.pallas_call(
        functools.partial(_proj_kernel, tuple(plan), tuple(nc if lay == "rows" else 0 for lay, nc, _ in outs)),
        grid=(n_tiles, m // rows.tm),
        in_specs=[pl.BlockSpec((rows.tm, k), lambda j, i: (i, 0)),
                  pl.BlockSpec((k, tn), lambda j, i: (0, j)),
                  pl.BlockSpec((rows.tm, LANE), lambda j, i: (i, 0)),
                  pl.BlockSpec((rows.tm, LANE), lambda j, i: (i, 0)),
                  pl.BlockSpec(gains.shape, lambda j, i: (0, 0))],
        out_specs=out_specs,
        out_shape=out_shape,
        compiler_params=_params(("arbitrary", "arbitrary")),
        name="in_projection",
    )(x, w, cos, sin, gains)


def _softmax_terms(s_list, m_list, guard):
    sm = [s if m is None else jnp.where(m, s, NEG) for s, m in zip(s_list, m_list)]
    mx = jnp.max(sm[0], axis=-1, keepdims=True)
    for x in sm[1:]:
        mx = jnp.maximum(mx, jnp.max(x, axis=-1, keepdims=True))
    es = [jnp.exp(x - mx) for x in sm]
    if guard:
        es = [jnp.where(m, e, 0.0) for e, m in zip(es, m_list)]
    tot = jnp.sum(es[0], axis=-1, keepdims=True)
    for e in es[1:]:
        tot = tot + jnp.sum(e, axis=-1, keepdims=True)
    inv = jnp.where(tot > 0.0, 1.0 / tot, 0.0) if guard else 1.0 / tot
    return es, inv


def _attend(pieces):
    s_list = [_dot_t(q, k) for q, k, _, _ in pieces]
    es, inv = _softmax_terms(s_list, [m for _, _, _, m in pieces], guard=False)
    o = _dot(es[0].astype(BF16), pieces[0][2])
    for e, (_, _, v, _) in zip(es[1:], pieces[1:]):
        o = o + _dot(e.astype(BF16), v)
    return o * inv


def _compress(z, wphi_ref, gk_ref, cosc_ref, sinc_ref, ck_ref, cv_ref):
    w0 = wphi_ref[0].astype(BF16)
    w1 = wphi_ref[1].astype(BF16)
    for g in range(NSA_KV_HEADS):
        zk = z[:, g * LANE:(g + 1) * LANE].astype(BF16)
        zv = z[:, (NSA_KV_HEADS + g) * LANE:(NSA_KV_HEADS + g + 1) * LANE].astype(BF16)
        ck = _rope(_rms(_dot(zk, w0), gk_ref[0:1, :]), cosc_ref[...], sinc_ref[...])
        ck_ref[g] = ck.astype(BF16)
        cv_ref[g] = _dot(zv, w1).astype(BF16)


def _pool(rows_f32, pe_t):
    n = rows_f32.shape[0] // CMP_BLOCK
    x3 = rows_f32.reshape(n, CMP_BLOCK, rows_f32.shape[1]) * pe_t[None]
    return jnp.sum(x3, axis=1) * (1.0 / CMP_BLOCK)


def _select_blocks(p_c, t1, q_rows, n_sel_blocks, n_live_blocks):
    qn = q_rows
    imp = p_c[0:qn] + p_c[qn:2 * qn] + p_c[2 * qn:3 * qn] + p_c[3 * qn:4 * qn]
    pair = imp + pltpu.roll(imp, LANE - 1, 1)
    lane = lax.broadcasted_iota(jnp.int32, (qn, LANE), 1)
    blk = lane >> 1
    lane_ok = ((lane & 1) == 0) & (lane < 2 * n_sel_blocks)
    cur = t1 >> 6
    forced = (blk == 0) | (blk == cur) | (blk == cur - 1)
    valid = lane_ok & ((blk << 6) <= t1)
    score = jnp.where(valid, pair + jnp.where(forced, FORCE, 0.0), -1.0)
    rank = jnp.zeros((qn, LANE), F32)
    for b in range(min(n_sel_blocks, n_live_blocks)):
        col = score[:, 2 * b:2 * b + 1]
        beats = (col > score) | ((col == score) & (lane > 2 * b))
        rank = rank + jnp.where(beats, 1.0, 0.0)
    sel = (rank < float(min(SEL_TOPK, n_sel_blocks))) & (score >= 0.0)
    return jnp.where(sel, 1.0, 0.0)


def _nsa_group(qg, t1, ck, cv, n_cmp, n_sel_blocks, n_live_blocks, sel_pieces, win_pieces):
    qn = t1.shape[0]
    t4 = jnp.concatenate([t1] * NSA_GROUP, axis=0)
    s = _dot_t(qg, ck)
    lane = lax.broadcasted_iota(jnp.int32, (1, NCMP_PAD), 1)
    cpos = jnp.where(lane < n_cmp, lane * CMP_BLOCK + (CMP_BLOCK - 1), jnp.int32(2 ** 30))
    es, inv = _softmax_terms([s], [cpos <= t4], guard=True)
    p_c = es[0] * inv
    o_c = _dot(p_c.astype(BF16), cv)
    sel = _select_blocks(p_c, t1, qn, n_sel_blocks, n_live_blocks)
    sel4 = jnp.concatenate([sel] * NSA_GROUP, axis=0)
    bias = jnp.where(sel4 > 0.5, 0.0, NEG)
    pieces = []
    for kind, k, v, extra in sel_pieces:
        if kind == "bias":
            b = bias if extra is None else jnp.minimum(bias, extra)
            pieces.append((jnp.concatenate([qg, b.astype(BF16)], axis=1), k, v, None))
        elif kind == "mask":
            pieces.append((jnp.concatenate([qg, bias.astype(BF16)], axis=1), k, v, extra <= t4))
        else:
            block_lane, kpos = extra
            flag = sel4[:, block_lane:block_lane + 1]
            pieces.append((qg, k, v, jnp.where(kpos <= t4, flag, 0.0) > 0.5))
    o_s = _attend(pieces)
    pieces = []
    for k, v, kpos in win_pieces:
        dist = t4 - kpos
        pieces.append((qg, k, v, (dist >= 0) & (dist < WINDOW) & (kpos >= 0)))
    o_w = _attend(pieces)
    return o_c, o_s, o_w


def _nsa_write(o_ref, gates, g, qn, o_c, o_s, o_w):
    for h in range(NSA_GROUP):
        hh = g * NSA_GROUP + h
        r = slice(h * qn, (h + 1) * qn)
        o = (gates[:, 3 * hh:3 * hh + 1] * o_c[r] + gates[:, 3 * hh + 1:3 * hh + 2] * o_s[r]
             + gates[:, 3 * hh + 2:3 * hh + 3] * o_w[r])
        o_ref[:, hh * LANE:(hh + 1) * LANE] = o.astype(o_ref.dtype)


def _lambda(lam_ref, lam_init):
    lv = lam_ref[...]
    return (jnp.exp(jnp.sum(lv[0:1] * lv[1:2], axis=-1, keepdims=True))
            - jnp.exp(jnp.sum(lv[2:3] * lv[3:4], axis=-1, keepdims=True)) + lam_init)


def _diff_head(q1, q2, pieces, lam, gain, lam_init):
    masks = [m for _, _, _, m in pieces]
    e1, inv1 = _softmax_terms([_dot_t(q1, k1) for k1, _, _, _ in pieces], masks, guard=False)
    e2, inv2 = _softmax_terms([_dot_t(q2, k2) for _, k2, _, _ in pieces], masks, guard=False)
    o = None
    for a1, a2, (_, _, v, _) in zip(e1, e2, pieces):
        a = a1 * inv1 - lam * (a2 * inv2)
        term = _dot(a.astype(BF16), v)
        o = term if o is None else o + term
    return _rms(o, gain) * (1.0 - lam_init)


Q_NSA = 128
Q_DIFF = 256
WIN_KEYS = WINDOW + Q_NSA
KEY_STEP = 256


def _nsa_prompt_kernel(seq, q_ref, g_ref, cmp_ref, kv_ref, pe_ref, wphi_ref, gk_ref, cosc_ref, sinc_ref,
                       et_ref, o_ref, ck_ref, cv_ref, kaug_ref):
    i = pl.program_id(1)

    @pl.when(i == 0)
    def _():
        z = _pool(cmp_ref[...], pe_ref[...])
        z = jnp.concatenate([z, jnp.zeros((NCMP_PAD - z.shape[0], z.shape[1]), F32)], axis=0)
        _compress(z, wphi_ref, gk_ref, cosc_ref, sinc_ref, ck_ref, cv_ref)
        for g in range(NSA_KV_HEADS):
            kaug_ref[g, :, 0:LANE] = kv_ref[:, g * LANE:(g + 1) * LANE]
            kaug_ref[g, :, LANE:2 * LANE] = et_ref[...]

    t0 = pl.multiple_of(i * Q_NSA, Q_NSA)
    t1 = t0 + lax.broadcasted_iota(jnp.int32, (Q_NSA, 1), 0)
    start = pl.multiple_of(jnp.clip(t0 - WINDOW, 0, seq - WIN_KEYS), Q_NSA)
    wpos = start + lax.broadcasted_iota(jnp.int32, (1, WIN_KEYS), 1)
    dpos = t0 + lax.broadcasted_iota(jnp.int32, (1, Q_NSA), 1)
    lane = lax.broadcasted_iota(jnp.int32, (1, LANE), 1)
    first_blk = t0 // SEL_BLOCK
    diag = (lane >= 2 * first_blk) & (lane < 2 * (first_blk + Q_NSA // SEL_BLOCK))
    drop_diag = jnp.where(diag, NEG, 0.0)
    gates = g_ref[...].astype(F32)

    def body(n_keys):
        for g in range(NSA_KV_HEADS):
            qg = jnp.concatenate([q_ref[:, (g * NSA_GROUP + h) * LANE:(g * NSA_GROUP + h + 1) * LANE]
                                  for h in range(NSA_GROUP)], axis=0)
            sel_pieces = [("bias", kaug_ref[g, 0:n_keys, :], kv_ref[0:n_keys, (2 + g) * LANE:(3 + g) * LANE],
                           drop_diag),
                          ("mask", kaug_ref[g, pl.ds(t0, Q_NSA), :],
                           kv_ref[pl.ds(t0, Q_NSA), (2 + g) * LANE:(3 + g) * LANE], dpos)]
            win_pieces = [(kv_ref[pl.ds(start, WIN_KEYS), (4 + g) * LANE:(5 + g) * LANE],
                           kv_ref[pl.ds(start, WIN_KEYS), (6 + g) * LANE:(7 + g) * LANE], wpos)]
            o_c, o_s, o_w = _nsa_group(qg, t1, ck_ref[g], cv_ref[g], seq // CMP_BLOCK, seq // SEL_BLOCK,
                                       n_keys // SEL_BLOCK, sel_pieces, win_pieces)
            _nsa_write(o_ref, gates, g, Q_NSA, o_c, o_s, o_w)

    need = (t0 + Q_NSA + KEY_STEP - 1) // KEY_STEP
    for nk in range(1, seq // KEY_STEP + 1):
        pl.when(need == nk)(functools.partial(body, nk * KEY_STEP))


def _nsa_prompt(q, gates, nsa_rows, kv_bf, pe_t, w_phi, g_nsa_k, cosc, sinc, e_mat, layer, batch, seq):
    m = q.shape[0]
    nq = seq // Q_NSA
    gate_blk = gates.shape[1] // LANE - 1
    return pl.pallas_call(
        functools.partial(_nsa_prompt_kernel, seq),
        grid=(batch, nq),
        in_specs=[pl.BlockSpec((Q_NSA, NSA_HEADS * HEAD_DIM), lambda b, i: (b * nq + i, 0)),
                  pl.BlockSpec((Q_NSA, LANE), lambda b, i: (b * nq + i, gate_blk)),
                  pl.BlockSpec((seq, 4 * LANE), lambda b, i: (b, 0)),
                  pl.BlockSpec((seq, 8 * LANE), lambda b, i: (b, 0)),
                  pl.BlockSpec((None, CMP_BLOCK, 4 * LANE), lambda b, i: (layer, 0, 0)),
                  pl.BlockSpec((None, 2, HEAD_DIM, HEAD_DIM), lambda b, i: (layer, 0, 0, 0)),
                  pl.BlockSpec((None, 3, HEAD_DIM), lambda b, i: (layer, 0, 0)),
                  pl.BlockSpec((NCMP_PAD, LANE), lambda b, i: (0, 0)),
                  pl.BlockSpec((NCMP_PAD, LANE), lambda b, i: (0, 0)),
                  pl.BlockSpec(e_mat.shape, lambda b, i: (0, 0))],
        out_specs=pl.BlockSpec((Q_NSA, NSA_HEADS * HEAD_DIM), lambda b, i: (b * nq + i, 0)),
        out_shape=jax.ShapeDtypeStruct((m, NSA_HEADS * HEAD_DIM), BF16),
        scratch_shapes=[pltpu.VMEM((NSA_KV_HEADS, NCMP_PAD, HEAD_DIM), BF16),
                        pltpu.VMEM((NSA_KV_HEADS, NCMP_PAD, HEAD_DIM), BF16),
                        pltpu.VMEM((NSA_KV_HEADS, seq, 2 * LANE), BF16)],
        compiler_params=_params(("arbitrary", "arbitrary")),
        name="nsa_prompt",
    )(q, gates, nsa_rows, kv_bf, pe_t, w_phi, g_nsa_k, cosc, sinc, e_mat)


def _diff_prompt_kernel(seq, lam_init, q_ref, kv_ref, lam_ref, gd_ref, o_ref):
    t0 = pl.program_id(1) * Q_DIFF
    t1 = t0 + lax.broadcasted_iota(jnp.int32, (Q_DIFF, 1), 0)
    lam = _lambda(lam_ref, lam_init)
    vbase = DIFF_HEADS * DIFF_V

    def body(n_keys):
        mask = lax.broadcasted_iota(jnp.int32, (1, n_keys), 1) <= t1
        for h in range(DIFF_HEADS):
            c = 2 * h * LANE
            pieces = [(kv_ref[0:n_keys, c:c + LANE], kv_ref[0:n_keys, c + LANE:c + 2 * LANE],
                       kv_ref[0:n_keys, vbase + h * DIFF_V:vbase + (h + 1) * DIFF_V], mask)]
            o = _diff_head(q_ref[:, c:c + LANE], q_ref[:, c + LANE:c + 2 * LANE], pieces, lam, gd_ref[...],
                           lam_init)
            o_ref[:, h * DIFF_V:(h + 1) * DIFF_V] = o.astype(o_ref.dtype)

    need = (t0 + Q_DIFF + KEY_STEP - 1) // KEY_STEP
    for nk in range(1, seq // KEY_STEP + 1):
        pl.when(need == nk)(functools.partial(body, nk * KEY_STEP))


def _diff_prompt(q, kv_bf, lam_diff, g_dout3, layer, lam_init, batch, seq):
    m = q.shape[0]
    nq = seq // Q_DIFF
    width = DIFF_HEADS * DIFF_V
    return pl.pallas_call(
        functools.partial(_diff_prompt_kernel, seq, lam_init),
        grid=(batch, nq),
        in_specs=[pl.BlockSpec((Q_DIFF, width), lambda b, i: (b * nq + i, 0)),
                  pl.BlockSpec((seq, 2 * width), lambda b, i: (b, 0)),
                  pl.BlockSpec((None, 4, DIFF_QK), lambda b, i: (layer, 0, 0)),
                  pl.BlockSpec((None, 1, DIFF_V), lambda b, i: (layer, 0, 0))],
        out_specs=pl.BlockSpec((Q_DIFF, width), lambda b, i: (b * nq + i, 0)),
        out_shape=jax.ShapeDtypeStruct((m, width), BF16),
        compiler_params=_params(("arbitrary", "arbitrary")),
        name="diff_prompt",
    )(q, kv_bf, lam_diff, g_dout3)


def _pad_rows(x, n):
    return jnp.concatenate([x, jnp.zeros((n - x.shape[0], x.shape[1]), x.dtype)], axis=0)


def _nsa_sample_kernel(n_pages, page, steps, pt_ref, *refs):
    pages = refs[:n_pages]
    (q_ref, g_ref, new_ref, wnew_ref, wbuf_ref, pe_ref, wphi_ref, gk_ref, cosc_ref, sinc_ref, et_ref,
     o_ref, kaug, vbuf, zbuf, ck_ref, cv_ref) = refs[n_pages:]
    past = n_pages * page
    per_page = page // CMP_BLOCK
    pe_t = pe_ref[...]

    @pl.when(pl.program_id(0) == 0)
    def _():
        for g in range(NSA_KV_HEADS):
            kaug[g, :, LANE:2 * LANE] = et_ref[...]

    zbuf[...] = jnp.zeros(zbuf.shape, F32)
    for p in range(n_pages):
        rows_p = slice(p * page, (p + 1) * page)
        for c in range(4):
            cs = slice(c * LANE, (c + 1) * LANE)
            zbuf[p * per_page:(p + 1) * per_page, cs] = _pool(pages[p][pl.ds(c, page, stride=8), :], pe_t[:, cs])
        for g in range(NSA_KV_HEADS):
            kaug[g, rows_p, 0:LANE] = pages[p][pl.ds(4 + g, page, stride=8), :].astype(BF16)
            vbuf[rows_p, g * LANE:(g + 1) * LANE] = pages[p][pl.ds(6 + g, page, stride=8), :].astype(BF16)
    new = new_ref[...]
    z_new = jnp.sum(new[:, 0:4 * LANE] * pe_t[0:steps, :], axis=0, keepdims=True) * (1.0 / CMP_BLOCK)
    zbuf[past // CMP_BLOCK:past // CMP_BLOCK + 1, :] = z_new
    _compress(zbuf[...], wphi_ref, gk_ref, cosc_ref, sinc_ref, ck_ref, cv_ref)

    padded = -(-(past + steps) // SEL_BLOCK) * SEL_BLOCK
    n_cmp = padded // CMP_BLOCK
    n_sel = padded // SEL_BLOCK
    t1 = past + lax.broadcasted_iota(jnp.int32, (steps, 1), 0)
    kpos_new = past + lax.broadcasted_iota(jnp.int32, (1, LANE), 1)
    kpos_new = jnp.where(kpos_new < past + steps, kpos_new, jnp.int32(2 ** 30))
    w_buf = wbuf_ref.shape[0] // 4
    wpos_past = (past - w_buf) + lax.broadcasted_iota(jnp.int32, (1, w_buf), 1)
    gates = g_ref[...]
    wnew = wnew_ref[...]
    q = q_ref[...]
    for g in range(NSA_KV_HEADS):
        qg = jnp.concatenate([q[:, (g * NSA_GROUP + h) * LANE:(g * NSA_GROUP + h + 1) * LANE]
                              for h in range(NSA_GROUP)], axis=0).astype(BF16)
        k_new = _pad_rows(new[:, (4 + g) * LANE:(5 + g) * LANE], LANE).astype(BF16)
        v_new = _pad_rows(new[:, (6 + g) * LANE:(7 + g) * LANE], LANE).astype(BF16)
        sel_pieces = [("bias", kaug[g], vbuf[:, g * LANE:(g + 1) * LANE], None),
                      ("lane", k_new, v_new, (2 * (past // SEL_BLOCK), kpos_new))]
        wk_new = _pad_rows(wnew[:, g * LANE:(g + 1) * LANE], LANE).astype(BF16)
        wv_new = _pad_rows(wnew[:, (2 + g) * LANE:(3 + g) * LANE], LANE).astype(BF16)
        win_pieces = [(wbuf_ref[pl.ds(g, w_buf, stride=4), :].astype(BF16),
                       wbuf_ref[pl.ds(2 + g, w_buf, stride=4), :].astype(BF16), wpos_past),
                      (wk_new, wv_new, kpos_new)]
        o_c, o_s, o_w = _nsa_group(qg, t1, ck_ref[g], cv_ref[g], n_cmp, n_sel, n_sel, sel_pieces, win_pieces)
        _nsa_write(o_ref, gates, g, steps, o_c, o_s, o_w)


def _nsa_sample(page_table, cache4, q3, gates3, new3, wnew3, win_state, pe_t, w_phi, g_nsa_k, cosc, sinc,
                e_mat, layer):
    n_seq, n_pages = page_table.shape
    page = cache4.shape[2] // 8
    steps = q3.shape[1]
    past = n_pages * page
    gate_blk = gates3.shape[2] // LANE - 1
    w_rows = win_state.shape[2]

    def page_spec(p):
        return pl.BlockSpec((None, None, 8 * page, LANE), lambda s, pt: (layer, pt[s, p], 0, 0))

    def seq_spec(width, blk=0):
        return pl.BlockSpec((None, steps, width), lambda s, pt: (s, 0, blk))

    def const_spec(shape):
        return pl.BlockSpec(shape, lambda s, pt: (0,) * len(shape))

    grid_spec = pltpu.PrefetchScalarGridSpec(
        num_scalar_prefetch=1,
        grid=(n_seq,),
        in_specs=[page_spec(p) for p in range(n_pages)] + [
            seq_spec(NSA_HEADS * HEAD_DIM), seq_spec(LANE, gate_blk), seq_spec(8 * LANE), seq_spec(4 * LANE),
            pl.BlockSpec((None, None, w_rows, LANE), lambda s, pt: (layer, s, 0, 0)),
            pl.BlockSpec((None, CMP_BLOCK, 4 * LANE), lambda s, pt: (layer, 0, 0)),
            pl.BlockSpec((None, 2, HEAD_DIM, HEAD_DIM), lambda s, pt: (layer, 0, 0, 0)),
            pl.BlockSpec((None, 3, HEAD_DIM), lambda s, pt: (layer, 0, 0)),
            const_spec((NCMP_PAD, LANE)), const_spec((NCMP_PAD, LANE)), const_spec(e_mat.shape)],
        out_specs=pl.BlockSpec((None, steps, NSA_HEADS * HEAD_DIM), lambda s, pt: (s, 0, 0)),
        scratch_shapes=[pltpu.VMEM((NSA_KV_HEADS, past, 2 * LANE), BF16),
                        pltpu.VMEM((past, NSA_KV_HEADS * LANE), BF16),
                        pltpu.VMEM((NCMP_PAD, 4 * LANE), F32),
                        pltpu.VMEM((NSA_KV_HEADS, NCMP_PAD, HEAD_DIM), BF16),
                        pltpu.VMEM((NSA_KV_HEADS, NCMP_PAD, HEAD_DIM), BF16)])
    return pl.pallas_call(
        functools.partial(_nsa_sample_kernel, n_pages, page, steps),
        grid_spec=grid_spec,
        out_shape=jax.ShapeDtypeStruct((n_seq, steps, NSA_HEADS * HEAD_DIM), F32),
        compiler_params=_params(("arbitrary",)),
        name="nsa_sample",
    )(page_table, *([cache4] * n_pages), q3, gates3, new3, wnew3, win_state, pe_t, w_phi, g_nsa_k, cosc, sinc,
      e_mat)


def _token_rows(ref3, j):
    tokens = ref3.shape[0]
    return ref3.reshape(tokens * 8, LANE)[pl.ds(j, tokens, stride=8), :]


def _diff_sample_kernel(n_pages, page, steps, lam_init, pt_ref, *refs):
    pages = refs[:2 * n_pages]
    q_ref, new_ref, lam_ref, gd_ref, o_ref, kvbuf = refs[2 * n_pages:]
    past = n_pages * page
    for p in range(n_pages):
        for c in range(2 * DIFF_HEADS * 2):
            slot, head, half = c // 8, (c // 2) % DIFF_HEADS, c % 2
            kvbuf[p * page:(p + 1) * page, c * LANE:(c + 1) * LANE] = _token_rows(
                pages[slot * n_pages + p], 4 * half + head).astype(BF16)
    t1 = past + lax.broadcasted_iota(jnp.int32, (steps, 1), 0)
    mask_past = lax.broadcasted_iota(jnp.int32, (1, past), 1) <= t1
    mask_new = (past + lax.broadcasted_iota(jnp.int32, (1, LANE), 1)) <= t1
    lam = _lambda(lam_ref, lam_init)
    new = new_ref[...]
    q = q_ref[...].astype(BF16)
    vbase = DIFF_HEADS * DIFF_V
    for h in range(DIFF_HEADS):
        c = 2 * h * LANE
        k1n = _pad_rows(new[:, c:c + LANE], LANE).astype(BF16)
        k2n = _pad_rows(new[:, c + LANE:c + 2 * LANE], LANE).astype(BF16)
        vn = _pad_rows(new[:, vbase + h * DIFF_V:vbase + (h + 1) * DIFF_V], LANE).astype(BF16)
        pieces = [(kvbuf[:, c:c + LANE], kvbuf[:, c + LANE:c + 2 * LANE],
                   kvbuf[:, vbase + h * DIFF_V:vbase + (h + 1) * DIFF_V], mask_past),
                  (k1n, k2n, vn, mask_new)]
        o = _diff_head(q[:, c:c + LANE], q[:, c + LANE:c + 2 * LANE], pieces, lam, gd_ref[...], lam_init)
        o_ref[:, h * DIFF_V:(h + 1) * DIFF_V] = o.astype(o_ref.dtype)


def _diff_sample(page_table, cache4, q3, new3, lam_diff, g_dout3, layer, lam_init):
    n_seq, n_pages = page_table.shape
    page = cache4.shape[2]
    steps = q3.shape[1]
    width = DIFF_HEADS * DIFF_V
    grid_spec = pltpu.PrefetchScalarGridSpec(
        num_scalar_prefetch=1,
        grid=(n_seq,),
        in_specs=[pl.BlockSpec((None, None, page, 8, LANE), functools.partial(
            lambda slot, p, s, pt: (layer, pt[s, p], 0, slot, 0), slot, p))
            for slot in range(2) for p in range(n_pages)] + [
            pl.BlockSpec((None, steps, width), lambda s, pt: (s, 0, 0)),
            pl.BlockSpec((None, steps, 2 * width), lambda s, pt: (s, 0, 0)),
            pl.BlockSpec((None, 4, DIFF_QK), lambda s, pt: (layer, 0, 0)),
            pl.BlockSpec((None, 1, DIFF_V), lambda s, pt: (layer, 0, 0))],
        out_specs=pl.BlockSpec((None, steps, width), lambda s, pt: (s, 0, 0)),
        scratch_shapes=[pltpu.VMEM((n_pages * page, 2 * width), BF16)])
    return pl.pallas_call(
        functools.partial(_diff_sample_kernel, n_pages, page, steps, lam_init),
        grid_spec=grid_spec,
        out_shape=jax.ShapeDtypeStruct((n_seq, steps, width), F32),
        compiler_params=_params(("arbitrary",)),
        name="diff_sample",
    )(page_table, *([cache4] * (2 * n_pages)), q3, new3, lam_diff, g_dout3)


def _merge_kernel(a_ref, b_ref, wa_ref, wb_ref, ga_ref, gb_ref, o_ref, wa_s, wb_s):
    @pl.when(pl.program_id(1) == 0)
    def _():
        wa_s[...] = wa_ref[...].astype(BF16)
        wb_s[...] = wb_ref[...].astype(BF16)

    oa = _dot(a_ref[...].astype(BF16), wa_s[...])
    ob = _dot(b_ref[...].astype(BF16), wb_s[...])
    o_ref[...] = (ga_ref[...] * oa + gb_ref[...] * ob).astype(o_ref.dtype)


def _merge(attn_a, attn_b, w_branch, gates, layer, rows):
    m, ka = attn_a.shape
    kb = attn_b.shape[1]
    d = w_branch.shape[3]
    tn = 1024
    nj = d // tn
    return pl.pallas_call(
        _merge_kernel,
        grid=(nj, m // rows.tm),
        in_specs=[pl.BlockSpec((rows.tm, ka), lambda j, i: (i, 0)),
                  pl.BlockSpec((rows.tm, kb), lambda j, i: (i, 0)),
                  pl.BlockSpec((None, None, ka, tn), lambda j, i: (layer, 0, 0, j)),
                  pl.BlockSpec((None, None, kb, tn), lambda j, i: (layer, 1, 0, j)),
                  pl.BlockSpec((rows.tm, tn), lambda j, i: (i, j)),
                  pl.BlockSpec((rows.tm, tn), lambda j, i: (i, nj + j))],
        out_specs=pl.BlockSpec((rows.tm, tn), lambda j, i: (i, j)),
        out_shape=jax.ShapeDtypeStruct((m, d), BF16),
        scratch_shapes=[pltpu.VMEM((ka, tn), BF16), pltpu.VMEM((kb, tn), BF16)],
        compiler_params=_params(("arbitrary", "arbitrary")),
        name="merge_branches",
    )(attn_a, attn_b, w_branch, w_branch, gates, gates)


def _resid_kernel(rows, cast_w, a_ref, w_ref, x_ref, gt_ref, o_ref, *scratch):
    if cast_w:
        w_s, = scratch

        @pl.when(pl.program_id(1) == 0)
        def _():
            w_s[...] = w_ref[...].astype(BF16)

        w = w_s[...]
    else:
        w = w_ref[...]
    gt = _rows_bcast(gt_ref[...], rows, o_ref.shape[1])
    o_ref[...] = x_ref[...] + gt * _dot(a_ref[...], w)


def _resid_proj(a, w, w_layer, x2, mod3, comp_blocks, rows, tn):
    m, k = a.shape
    n = x2.shape[1]
    cast_w = w_layer is not None
    if cast_w:
        w_spec = pl.BlockSpec((None, k, tn), lambda j, i: (w_layer, 0, j))
        scratch = [pltpu.VMEM((k, tn), BF16)]
    else:
        w_spec = pl.BlockSpec((k, tn), lambda j, i: (0, j))
        scratch = []
    off = comp_blocks * (n // tn)
    return pl.pallas_call(
        functools.partial(_resid_kernel, rows, cast_w),
        grid=(n // tn, m // rows.tm),
        in_specs=[pl.BlockSpec((rows.tm, k), lambda j, i: (i, 0)),
                  w_spec,
                  pl.BlockSpec((rows.tm, tn), lambda j, i: (i, j)),
                  pl.BlockSpec((rows.bb, 1, tn), lambda j, i: (i // rows.mod_div, 0, off + j))],
        out_specs=pl.BlockSpec((rows.tm, tn), lambda j, i: (i, j)),
        out_shape=jax.ShapeDtypeStruct((m, n), F32),
        scratch_shapes=scratch,
        compiler_params=_params(("arbitrary", "arbitrary")),
        name="residual_projection",
    )(a, w, x2, mod3)


def _ffn_up_kernel(rows, carry_prev, h_ref, wg_ref, wv_ref, cw_ref, cb_ref, prev_ref, act_ref, st_ref, *scratch):
    tn = act_ref.shape[1]
    h = h_ref[...]
    if carry_prev:
        carry, = scratch

        @pl.when(pl.program_id(1) % rows.mod_div == 0)
        def _():
            carry[...] = prev_ref[...]

        prev_all = carry[...]
    else:
        prev_all = prev_ref[...]
    cw_all = cw_ref[...]
    cb_all = cb_ref[...]
    sub = FF_SUB
    for c0 in range(0, tn, sub):
        cs = slice(c0, c0 + sub)
        ug = _dot(h, wg_ref[:, cs])
        uv = _dot(h, wv_ref[:, cs])
        ug3 = ug.reshape(rows.bb, rows.tt, sub)
        p0 = prev_all[:, 0:1, cs]
        p1 = prev_all[:, 1:2, cs]
        rid = lax.broadcasted_iota(jnp.int32, ug3.shape, 1)
        u1 = jnp.where(rid == 0, p1, pltpu.roll(ug3, 1, 1))
        u2 = jnp.where(rid == 0, p0, jnp.where(rid == 1, p1, pltpu.roll(ug3, 2, 1)))
        cw = cw_all[:, cs]
        conv = cb_all[:, cs] + u2 * cw[0:1, :] + u1 * cw[1:2, :] + ug3 * cw[2:3, :]
        act = jax.nn.gelu(conv).reshape(rows.tm, sub) * uv
        act_ref[:, cs] = act.astype(act_ref.dtype)
        tail = ug3[:, rows.tt - (CONV_W - 1):rows.tt, :]
        st_ref[:, :, cs] = tail
        if carry_prev:
            carry[:, :, cs] = tail


def _ffn_up(h, w_up_p, conv_w_p, conv_b_p, prev, rows, carry_prev):
    m, k = h.shape
    fp = w_up_p.shape[1] // 2
    tn = TN_FF
    nj = fp // tn
    n_seq = prev.shape[0]
    scratch = [pltpu.VMEM((1, CONV_W - 1, tn), F32)] if carry_prev else []
    return pl.pallas_call(
        functools.partial(_ffn_up_kernel, rows, carry_prev),
        grid=(nj, m // rows.tm),
        in_specs=[pl.BlockSpec((rows.tm, k), lambda j, i: (i, 0)),
                  pl.BlockSpec((k, tn), lambda j, i: (0, j)),
                  pl.BlockSpec((k, tn), lambda j, i: (0, nj + j)),
                  pl.BlockSpec((CONV_W, tn), lambda j, i: (0, j)),
                  pl.BlockSpec((1, tn), lambda j, i: (0, j)),
                  pl.BlockSpec((rows.bb, CONV_W - 1, tn), lambda j, i: (i // rows.mod_div, 0, j))],
        out_specs=[pl.BlockSpec((rows.tm, tn), lambda j, i: (i, j)),
                   pl.BlockSpec((rows.bb, CONV_W - 1, tn), lambda j, i: (i // rows.mod_div, 0, j))],
        out_shape=[jax.ShapeDtypeStruct((m, fp), BF16),
                   jax.ShapeDtypeStruct((n_seq, CONV_W - 1, fp), F32)],
        scratch_shapes=scratch,
        compiler_params=_params(("arbitrary", "arbitrary")),
        name="ffn_up_conv",
    )(h, w_up_p, w_up_p, conv_w_p, conv_b_p, prev)


def _rope_tables(pos):
    half = HEAD_DIM // 2
    freqs = ROPE_THETA ** (-jnp.arange(half, dtype=F32) / half)
    ang = pos.astype(F32)[:, None] * freqs
    cos, sin = jnp.cos(ang), jnp.sin(ang)
    return jnp.concatenate([cos, cos], axis=-1), jnp.concatenate([-sin, sin], axis=-1)


def _proj_plans(sample):
    q_scale = HEAD_DIM ** -0.5
    d_scale = DIFF_QK ** -0.5
    plan_a = [("nr", 0, q_scale, ((0, c),)) for c in range(8)]
    if sample:
        plan_a += [("id", 0, 1.0, ((1, c), (3, c))) for c in range(4)]
        plan_a += [("nr", 3, 1.0, ((1, 4 + c), (3, 4 + c))) for c in range(2)]
        plan_a += [("id", 0, 1.0, ((1, 6 + c), (3, 6 + c))) for c in range(2)]
        plan_a += [("nr", 4, 1.0, ((2, c), (4, c))) for c in range(2)]
        plan_a += [("id", 0, 1.0, ((2, 2 + c), (4, 2 + c))) for c in range(2)]
    else:
        plan_a += [("id", 0, 1.0, ((1, c), (4, c))) for c in range(4)]
        plan_a += [("nr", 3, 1.0, ((1, 4 + c), (3, c))) for c in range(2)]
        plan_a += [("id", 0, 1.0, ((1, 6 + c), (3, 2 + c))) for c in range(2)]
        plan_a += [("nr", 4, 1.0, ((2, c), (3, 4 + c))) for c in range(2)]
        plan_a += [("id", 0, 1.0, ((2, 2 + c), (3, 6 + c))) for c in range(2)]
    plan_b = [("nr", 5, d_scale, ((0, c),)) for c in range(8)]
    plan_b += [("nr", 6, 1.0, ((1, 4 * (c % 2) + c // 2), (2, c))) for c in range(8)]
    plan_b += [("id", 0, 1.0, ((1, 8 + 4 * (c % 2) + c // 2), (2, 8 + c))) for c in range(8)]
    return plan_a, plan_b


def _win_shift_kernel(st_ref, new_ref, o_ref):
    total = st_ref.shape[1]
    fresh = new_ref.shape[1]
    o_ref[:, 0:total - fresh, :] = st_ref[:, fresh:total, :]
    o_ref[:, total - fresh:total, :] = new_ref[...]


def _win_shift(win_state4, new_rows):
    depth, n_seq, total, _ = win_state4.shape
    fresh = new_rows.shape[2]
    bs = 8 if n_seq % 8 == 0 else 4
    return pl.pallas_call(
        _win_shift_kernel,
        grid=(depth, n_seq // bs),
        in_specs=[pl.BlockSpec((None, bs, total, LANE), lambda l, s: (l, s, 0, 0)),
                  pl.BlockSpec((None, bs, fresh, LANE), lambda l, s: (l, s, 0, 0))],
        out_specs=pl.BlockSpec((None, bs, total, LANE), lambda l, s: (l, s, 0, 0)),
        out_shape=jax.ShapeDtypeStruct(win_state4.shape, F32),
        compiler_params=_params(("arbitrary", "arbitrary")),
        name="window_shift",
    )(win_state4, new_rows)


def kernel(x_prompt, x_sample, cache_nsa_kv, cache_diff_kv, state_win_kv, state_conv, page_table, c_prompt,
           c_sample, w_ada, b_ada, norm1_g, norm2_g, w_in, g_nsa_q, g_nsa_k, pe_cmp, w_phi, g_diff_q, g_diff_k,
           lam_diff, g_diff_out, w_branch, w_out, w_up, conv_w, conv_b, w_down):
    batch, seq, d = x_prompt.shape
    n_seq, steps, _ = x_sample.shape
    depth = w_in.shape[0]
    n_pages, page = page_table.shape[1], cache_nsa_kv.shape[2]
    past = n_pages * page
    n_pool = cache_nsa_kv.shape[1]
    d_ff = w_down.shape[1]
    fp = -(-d_ff // TN_FF) * TN_FF

    rows_p = Rows(batch * seq, TM, 1, TM, seq // TM)
    rows_pf = Rows(batch * seq, TM_FF, 1, TM_FF, seq // TM_FF)
    tm_s = min(TM, n_seq * steps)
    rows_s = Rows(n_seq * steps, tm_s, tm_s // steps, steps, 1)

    pos_p = jnp.tile(jnp.arange(seq), batch)
    pos_s = jnp.tile(past + jnp.arange(steps), n_seq)
    cos_p, sin_p = _rope_tables(pos_p)
    cos_s, sin_s = _rope_tables(pos_s)
    cosc, sinc = _rope_tables(jnp.arange(NCMP_PAD) * CMP_BLOCK + (CMP_BLOCK - 1))
    e_p = (jnp.arange(LANE)[None, :] == 2 * (jnp.arange(seq)[:, None] // SEL_BLOCK)).astype(BF16)
    e_s = (jnp.arange(LANE)[None, :] == 2 * (jnp.arange(past)[:, None] // SEL_BLOCK)).astype(BF16)
    assert past % SEL_BLOCK == 0 and steps <= SEL_BLOCK and 2 * (past // SEL_BLOCK) < LANE

    mb = -(-(batch + n_seq) // 8) * 8
    c_all = jnp.pad(jnp.concatenate([c_prompt, c_sample], axis=0), ((0, mb - batch - n_seq), (0, 0)))
    b_ada3 = b_ada[:, None, :]
    norm1_3 = norm1_g[:, None, :]
    norm2_3 = norm2_g[:, None, :]
    g_dout3 = g_diff_out[:, None, :]
    pe_t = jnp.concatenate([pe_cmp[:, 0], pe_cmp[:, 0], pe_cmp[:, 1], pe_cmp[:, 1]], axis=-1)
    cache_nsa4 = cache_nsa_kv.reshape(depth, n_pool, page * 8, LANE)
    cache_diff4 = cache_diff_kv.reshape(depth, n_pool, page, 2, DIFF_HEADS, 2, LANE).transpose(
        0, 1, 2, 3, 5, 4, 6).reshape(depth, n_pool, page, 16, LANE)
    win_state4 = state_win_kv.reshape(depth, n_seq, state_win_kv.shape[2] * 4, LANE)
    plan_g = [("sig", 0, 1.0, ((0, c),)) for c in range(11)]

    n_q = NSA_HEADS * HEAD_DIM
    n_kv = 6 * NSA_KV_HEADS * HEAD_DIM
    n_g = 3 * NSA_HEADS
    n_qd = DIFF_HEADS * 2 * DIFF_QK
    o_g = n_q + n_kv
    o_d = o_g + n_g
    o_m = o_d + 3 * n_qd

    xp = x_prompt.reshape(batch * seq, d)
    xs = x_sample.reshape(n_seq * steps, d)
    outs = [[] for _ in range(8)]
    for l in range(depth):
        lam_init = 0.8 - 0.6 * math.exp(-0.3 * l)
        wl = w_in[l]
        w_a = wl[:, :o_g].astype(BF16)
        w_b = wl[:, o_d:o_m].astype(BF16)
        w_g = jnp.concatenate([wl[:, o_m:], wl[:, o_g:o_d],
                               jnp.zeros((d, LANE - n_g), F32)], axis=1).astype(BF16)
        gains = jnp.stack([g_nsa_q[l], g_nsa_q[l], g_nsa_q[l], g_nsa_k[l, 1], g_nsa_k[l, 2],
                           g_diff_q[l], g_diff_k[l], g_diff_k[l]])
        pad_f = ((0, 0), (0, fp - d_ff))
        w_up_p = jnp.concatenate([jnp.pad(w_up[l][:, :d_ff], pad_f), jnp.pad(w_up[l][:, d_ff:], pad_f)],
                                 axis=1).astype(BF16)
        w_down_p = jnp.pad(w_down[l], ((0, fp - d_ff), (0, 0))).astype(BF16)
        conv_w_p = jnp.pad(conv_w[l], pad_f)
        conv_b_p = jnp.pad(conv_b[l][None, :], pad_f)

        mod = _ada(c_all, w_ada, b_ada3, l)
        mod_p = mod[:batch, None, :]
        mod_s = mod[batch:batch + n_seq, None, :]

        new_x = []
        for grp, (x2, rows, mod3, cos, sin) in enumerate(((xp, rows_p, mod_p, cos_p, sin_p),
                                                          (xs, rows_s, mod_s, cos_s, sin_s))):
            sample = grp == 1
            plan_a, plan_b = _proj_plans(sample)
            h = _norm_mod(x2, norm1_3, mod3, l, 0, rows)
            if sample:
                outs_a = [("cols", 8, F32), ("rows", 8, F32), ("rows", 4, F32), ("cols", 8, F32), ("cols", 4, F32)]
                outs_b = [("cols", 8, F32), ("rows", 16, F32), ("cols", 16, F32)]
            else:
                outs_a = [("cols", 8, BF16), ("rows", 8, F32), ("rows", 4, F32), ("cols", 8, BF16), ("cols", 4, F32)]
                outs_b = [("cols", 8, BF16), ("rows", 16, F32), ("cols", 16, BF16)]
            q_nsa, nsa_tm, win_tm, kv_a, kv_b = _proj(h, w_a, cos, sin, gains, plan_a, outs_a, rows)
            q_diff, diff_tm, dkv = _proj(h, w_b, cos, sin, gains, plan_b, outs_b, rows)
            gates, = _proj(h, w_g, cos, sin, gains, plan_g, [("cols", 11, F32 if sample else BF16)], rows,
                           n_tiles=3)
            if not sample:
                a_nsa = _nsa_prompt(q_nsa, gates, kv_b, kv_a, pe_t, w_phi, g_nsa_k, cosc, sinc, e_p,
                                    l, batch, seq)
                a_diff = _diff_prompt(q_diff, dkv, lam_diff, g_dout3, l, lam_init, batch, seq)
            else:
                a_nsa = _nsa_sample(page_table, cache_nsa4, q_nsa.reshape(n_seq, steps, -1),
                                    gates.reshape(n_seq, steps, -1), kv_a.reshape(n_seq, steps, -1),
                                    kv_b.reshape(n_seq, steps, -1), win_state4, pe_t, w_phi, g_nsa_k,
                                    cosc, sinc, e_s, l).reshape(n_seq * steps, -1)
                a_diff = _diff_sample(page_table, cache_diff4, q_diff.reshape(n_seq, steps, -1),
                                      dkv.reshape(n_seq, steps, -1), lam_diff, g_dout3, l,
                                      lam_init).reshape(n_seq * steps, -1)
            merged = _merge(a_nsa, a_diff, w_branch, gates, l, rows)
            x_mid = _resid_proj(merged, w_out, l, x2, mod3, 2, rows, 1024)
            h2 = _norm_mod(x_mid, norm2_3, mod3, l, 3, rows)
            if not sample:
                prev = jnp.zeros((batch, CONV_W - 1, fp), F32)
            else:
                prev = jnp.pad(state_conv[l], ((0, 0), (0, 0), (0, fp - d_ff)))
            rows_f = rows if sample else rows_pf
            act, conv_state = _ffn_up(h2, w_up_p, conv_w_p, conv_b_p, prev, rows_f, carry_prev=not sample)
            x_new = _resid_proj(act, w_down_p, None, x_mid, mod3, 5, rows_f, 512)
            new_x.append(x_new)

            nb, nt = (n_seq, steps) if sample else (batch, seq)
            r_nsa = nsa_tm.reshape(nb, nt, 4, NSA_KV_HEADS, HEAD_DIM)
            r_diff = diff_tm.reshape(nb, nt, 2, 2, DIFF_HEADS, LANE).transpose(0, 1, 2, 4, 3, 5).reshape(
                nb, nt, 2, DIFF_HEADS, DIFF_V)
            if sample:
                r_win = win_tm.reshape(n_seq, steps * 4, LANE)
            else:
                r_win = win_tm.reshape(nb, nt, 2, NSA_KV_HEADS, HEAD_DIM)[:, nt - min(WINDOW, nt):]
            for slot, val in zip(range(4), (r_nsa, r_diff, r_win, conv_state[:, :, :d_ff])):
                outs[4 * grp + slot].append(val)
        xp, xs = new_x
    outs = [jnp.stack(o) for o in outs]
    outs[6] = _win_shift(win_state4, outs[6]).reshape(state_win_kv.shape)
    return (xp.reshape(batch, seq, d), xs.reshape(n_seq, steps, d)) + tuple(outs)
```
